```python
import jax, jax.numpy as jnp
from jax import lax
import numpy as np

D_MODEL = 1024
BATCH = 8
SEQ = 4096
DEPTH = 1

A_HEADS = 8
A_HEAD_DIM = 64
A_WIDTH = A_HEADS * A_HEAD_DIM
DILATED_PATTERNS = ((128, 1), (512, 4), (2048, 16))
BLOCK = 128
B_HEADS = 8
B_NOPE_DIM = 64
B_ROPE_DIM = 32
B_V_DIM = 64
B_WIDTH = B_HEADS * B_V_DIM
Q_LORA_RANK = 256
KV_LORA_RANK = 128
ROPE_THETA = 10000.0
MIX_WIDTH = A_WIDTH + B_WIDTH
IN_SPLITS = (A_WIDTH, A_WIDTH, A_WIDTH, A_WIDTH, Q_LORA_RANK, KV_LORA_RANK, B_ROPE_DIM, B_WIDTH)
IN_WIDTH = A_WIDTH * 4 + Q_LORA_RANK + KV_LORA_RANK + B_ROPE_DIM + B_WIDTH
LN_EPS = 1e-5
RMS_EPS = 1e-6
DEEPNORM_ALPHA = (2 * DEPTH) ** 0.25
DEEPNORM_BETA = (8 * DEPTH) ** -0.25

kernel_name = "hybrid_dilated_mla_deepnorm"


def _split(h, sizes):
    out, start = [], 0
    for sz in sizes:
        out.append(h[..., start:start + sz])
        start += sz
    return out


def _rmsnorm(t, g):
    t32 = t.astype(jnp.float32)
    t32 = t32 * lax.rsqrt(jnp.mean(t32 * t32, axis=-1, keepdims=True) + RMS_EPS)
    return (t32 * g.astype(jnp.float32)).astype(t.dtype)


def _layernorm(t, g, b):
    t32 = t.astype(jnp.float32)
    mu = jnp.mean(t32, axis=-1, keepdims=True)
    var = jnp.mean(jnp.square(t32 - mu), axis=-1, keepdims=True)
    y = (t32 - mu) * lax.rsqrt(var + LN_EPS)
    return y * g.astype(jnp.float32) + b.astype(jnp.float32)


def _alibi_slopes(n_heads):
    return 2.0 ** (-8.0 * jnp.arange(1, n_heads + 1, dtype=jnp.float32) / n_heads)


def _rope_tables(seq_len, dim):
    inv_freq = ROPE_THETA ** (-jnp.arange(0, dim, 2, dtype=jnp.float32) / dim)
    ang = jnp.arange(seq_len, dtype=jnp.float32)[:, None] * inv_freq[None, :]
    ang = jnp.concatenate([ang, ang], axis=-1)
    return jnp.cos(ang), jnp.sin(ang)


def _apply_rope(t, cos, sin):
    half = t.shape[-1] // 2
    rot = jnp.concatenate([-t[..., half:], t[..., :half]], axis=-1)
    return t * cos + rot * sin


def _dilated_pattern(q, k, v, slopes, window, dilation):
    bsz, S, H, Dh = q.shape
    W = window // dilation
    n = S // dilation
    nb = -(-n // BLOCK)
    n_pad = nb * BLOCK
    n_prev = -(-W // BLOCK)
    span = (n_prev + 1) * BLOCK

    def residues(t):
        return t.reshape(bsz, n, dilation, H, Dh).transpose(0, 2, 1, 3, 4)

    qs = jnp.pad(residues(q), ((0, 0), (0, 0), (0, n_pad - n), (0, 0), (0, 0)))
    qs = qs.reshape(bsz, dilation, nb, BLOCK, H, Dh)

    def key_band(t):
        t = jnp.pad(residues(t), ((0, 0), (0, 0), (n_prev * BLOCK, n_pad - n), (0, 0), (0, 0)))
        t = t.reshape(bsz, dilation, nb + n_prev, BLOCK, H, Dh)
        return jnp.concatenate([t[:, :, i:i + nb] for i in range(n_prev + 1)], axis=3)

    kw, vw = key_band(k), key_band(v)

    qi = jnp.arange(BLOCK)[:, None]
    kj = jnp.arange(span)[None, :]
    rel = qi + n_prev * BLOCK - kj
    kpos = jnp.arange(nb)[:, None, None] * BLOCK - n_prev * BLOCK + kj[None]
    valid = (rel >= 0)[None] & (rel <= W)[None] & (kpos >= 0)
    bias = -slopes[:, None, None] * (rel * dilation).astype(jnp.float32)[None]

    s = jnp.einsum('brnqhd,brnkhd->brnhqk', qs, kw) * (Dh ** -0.5) + bias
    s = jnp.where(valid[:, None], s, -jnp.inf)
    m = jnp.max(s, axis=-1)
    p = jnp.exp(s - m[..., None])
    l = jnp.sum(p, axis=-1)
    acc = jnp.einsum('brnhqk,brnkhd->brnqhd', p, vw)

    def back(t):
        t = t.reshape((bsz, dilation, n_pad) + t.shape[4:])[:, :, :n]
        t = jnp.moveaxis(t, 1, 2)
        return t.reshape((bsz, S) + t.shape[3:])

    return back(acc), back(m.transpose(0, 1, 2, 4, 3)), back(l.transpose(0, 1, 2, 4, 3))


def _dilated_mixture(q, k, v):
    slopes = _alibi_slopes(q.shape[2])
    parts = [_dilated_pattern(q, k, v, slopes, w, d) for (w, d) in DILATED_PATTERNS]
    m_all = jnp.max(jnp.stack([pm for (_, pm, _) in parts], axis=0), axis=0)
    num = 0.0
    den = 0.0
    for acc, pm, pl in parts:
        wgt = jnp.exp(pm - m_all)
        num = num + acc * wgt[..., None]
        den = den + pl * wgt
    return num / den[..., None]


def _mla_attention(q_nope, q_rope, k_nope, k_rope, v):
    bsz, S, H, _ = q_nope.shape
    nb = S // BLOCK
    scale = (B_NOPE_DIM + B_ROPE_DIM) ** -0.5
    qn = jnp.moveaxis(q_nope.reshape(bsz, nb, BLOCK, H, -1), 1, 0)
    qr = jnp.moveaxis(q_rope.reshape(bsz, nb, BLOCK, H, -1), 1, 0)
    kpos = jnp.arange(S)

    def block(args):
        qn_b, qr_b, b = args
        s = (jnp.einsum('bqhd,bkhd->bhqk', qn_b, k_nope)
             + jnp.einsum('bqhd,bkd->bhqk', qr_b, k_rope)) * scale
        qpos = b * BLOCK + jnp.arange(BLOCK)
        s = jnp.where((kpos[None, :] <= qpos[:, None])[None, None], s, -jnp.inf)
        p = jax.nn.softmax(s, axis=-1)
        return jnp.einsum('bhqk,bkhd->bqhd', p, v)

    o = lax.map(block, (qn, qr, jnp.arange(nb)))
    return jnp.moveaxis(o, 0, 1).reshape(bsz, S, H, -1)


def setup_inputs(seed: int = 0) -> dict:
    key = jax.random.key(seed)
    ks = jax.random.split(key, 9)
    f32 = jnp.float32
    x = jax.random.normal(ks[0], (BATCH, SEQ, D_MODEL), f32)
    w_in = jax.random.normal(ks[1], (D_MODEL, IN_WIDTH), f32) * D_MODEL ** -0.5
    q_norm_g = 1.0 + 0.02 * jax.random.normal(ks[2], (Q_LORA_RANK,), f32)
    w_uq = jax.random.normal(ks[3], (Q_LORA_RANK, B_HEADS * (B_NOPE_DIM + B_ROPE_DIM)), f32) * Q_LORA_RANK ** -0.5
    kv_norm_g = 1.0 + 0.02 * jax.random.normal(ks[4], (KV_LORA_RANK,), f32)
    w_ukv = jax.random.normal(ks[5], (KV_LORA_RANK, B_HEADS * (B_NOPE_DIM + B_V_DIM)), f32) * KV_LORA_RANK ** -0.5
    w_o = jax.random.normal(ks[6], (MIX_WIDTH, D_MODEL), f32) * (MIX_WIDTH ** -0.5) * DEEPNORM_BETA
    ln_g = 1.0 + 0.02 * jax.random.normal(ks[7], (D_MODEL,), f32)
    ln_b = 0.02 * jax.random.normal(ks[8], (D_MODEL,), f32)
    return {"x": x, "w_in": w_in, "q_norm_g": q_norm_g, "w_uq": w_uq, "kv_norm_g": kv_norm_g,
            "w_ukv": w_ukv, "w_o": w_o, "ln_g": ln_g, "ln_b": ln_b}


def reference(x, w_in, q_norm_g, w_uq, kv_norm_g, w_ukv, w_o, ln_g, ln_b):
    bsz, S, _ = x.shape
    f32 = jnp.float32
    cos, sin = _rope_tables(S, B_ROPE_DIM)
    for _ in range(DEPTH):
        h = x @ w_in
        a_q, a_k, a_v, a_gate, c_q, c_kv, k_rope, b_gate = _split(h, IN_SPLITS)

        shp = (bsz, S, A_HEADS, A_HEAD_DIM)
        a_out = _dilated_mixture(a_q.reshape(shp).astype(f32), a_k.reshape(shp).astype(f32),
                                 a_v.reshape(shp).astype(f32))
        a_out = a_out.reshape(bsz, S, A_WIDTH).astype(x.dtype) * jax.nn.silu(a_gate)

        q = (_rmsnorm(c_q, q_norm_g) @ w_uq).reshape(bsz, S, B_HEADS, B_NOPE_DIM + B_ROPE_DIM)
        kv = (_rmsnorm(c_kv, kv_norm_g) @ w_ukv).reshape(bsz, S, B_HEADS, B_NOPE_DIM + B_V_DIM)
        q_nope = q[..., :B_NOPE_DIM].astype(f32)
        q_rope = _apply_rope(q[..., B_NOPE_DIM:].astype(f32), cos[:, None, :], sin[:, None, :])
        k_nope = kv[..., :B_NOPE_DIM].astype(f32)
        v = kv[..., B_NOPE_DIM:].astype(f32)
        k_r = _apply_rope(k_rope.astype(f32), cos, sin)
        b_out = _mla_attention(q_nope, q_rope, k_nope, k_r, v)
        b_out = b_out.reshape(bsz, S, B_WIDTH).astype(x.dtype) * jax.nn.silu(b_gate)

        y = jnp.concatenate([a_out, b_out], axis=-1) @ w_o
        x = _layernorm(DEEPNORM_ALPHA * x.astype(f32) + y.astype(f32), ln_g, ln_b).astype(x.dtype)
    return x
```

```python
import functools

import jax
import jax.numpy as jnp
import numpy as np
from jax import lax
from jax.experimental import pallas as pl
from jax.experimental.pallas import tpu as pltpu

D_MODEL = 1024
A_HEADS = 8
A_HEAD_DIM = 64
A_WIDTH = A_HEADS * A_HEAD_DIM
DILATED_PATTERNS = ((128, 1), (512, 4), (2048, 16))
BAND = 128
B_HEADS = 8
B_NOPE_DIM = 64
B_ROPE_DIM = 32
B_V_DIM = 64
B_WIDTH = B_HEADS * B_V_DIM
Q_LORA_RANK = 256
KV_LORA_RANK = 128
ROPE_THETA = 10000.0
LN_EPS = 1e-5
RMS_EPS = 1e-6
DEPTH = 1
DEEPNORM_ALPHA = (2 * DEPTH) ** 0.25

LANES = 128
HEAD_PAIRS = A_HEADS // 2
NEG = -1e30
VMEM_LIMIT = 56 * 1024 * 1024

BF16 = jnp.bfloat16
F32 = jnp.float32

C_AQKV = 0
C_GATE = 1536
C_CQ = 2560
C_CKV = 2816
C_KR = 2944
IN_COLS = 3072


def _dot(a, b):
    return jnp.dot(a, b, preferred_element_type=F32)


def _dot_nt(a, b):
    return lax.dot_general(a, b, (((1,), (1,)), ((), ())), preferred_element_type=F32)


def _rms(t, g):
    return t * lax.rsqrt(jnp.mean(t * t, axis=-1, keepdims=True) + RMS_EPS) * g


def _silu(t):
    return t / (1.0 + jnp.exp(-t))


def _proj_kernel(x_ref, win_ref, wq_ref, wkv_ref, place_ref, gq_ref, gkv_ref, csk_ref, cosq_ref, sinq_ref,
                 aqkv_ref, gate_ref, qf_ref, kf_ref, v_ref):
    xb = x_ref[...].astype(BF16)
    aqkv_ref[...] = _dot(xb, win_ref[:, C_AQKV:C_GATE]).astype(BF16)
    gate_ref[...] = _silu(_dot(xb, win_ref[:, C_GATE:C_CQ])).astype(BF16)
    lat = _dot(xb, win_ref[:, C_CQ:IN_COLS])
    cq = lat[:, 0:Q_LORA_RANK]
    ckv = lat[:, Q_LORA_RANK:Q_LORA_RANK + KV_LORA_RANK]
    rope = lat[:, Q_LORA_RANK + KV_LORA_RANK:]
    q2 = _dot(_rms(cq, gq_ref[...]).astype(BF16), wq_ref[...])
    qf_ref[...] = (q2[:, :B_HEADS * LANES] * cosq_ref[...] + q2[:, B_HEADS * LANES:] * sinq_ref[...]).astype(BF16)
    r2 = rope * csk_ref[...]
    kr = r2 + pltpu.roll(r2, LANES - B_ROPE_DIM, axis=1)
    kv = _dot(_rms(ckv, gkv_ref[...]).astype(BF16), wkv_ref[...])
    kf_ref[...] = (kv[:, :B_HEADS * LANES] + _dot(kr.astype(BF16), place_ref[...])).astype(BF16)
    v_ref[...] = kv[:, B_HEADS * LANES:].astype(BF16)


def _project(x, wts, tabs, tm):
    bsz, seq, _ = x.shape
    row = lambda w: pl.BlockSpec((None, tm, w), lambda i, b: (b, i, 0))
    full = lambda a: pl.BlockSpec(a.shape, lambda i, b: (0,) * a.ndim)
    tab = lambda w: pl.BlockSpec((tm, w), lambda i, b: (i, 0))
    out = lambda w: jax.ShapeDtypeStruct((bsz, seq, w), BF16)
    return pl.pallas_call(
        _proj_kernel,
        grid=(seq // tm, bsz),
        in_specs=[row(D_MODEL), full(wts["win"]), full(wts["wq"]), full(wts["wkv"]), full(wts["place"]),
                  full(wts["gq"]), full(wts["gkv"]), tab(LANES), tab(B_HEADS * LANES), tab(B_HEADS * LANES)],
        out_specs=[row(3 * A_WIDTH), row(A_WIDTH + B_WIDTH), row(B_HEADS * LANES), row(B_HEADS * LANES), row(B_WIDTH)],
        out_shape=[out(3 * A_WIDTH), out(A_WIDTH + B_WIDTH), out(B_HEADS * LANES), out(B_HEADS * LANES), out(B_WIDTH)],
        compiler_params=pltpu.CompilerParams(dimension_semantics=("arbitrary", "arbitrary"),
                                             vmem_limit_bytes=VMEM_LIMIT),
        name="proj",
    )(x, wts["win"], wts["wq"], wts["wkv"], wts["place"], wts["gq"], wts["gkv"],
      tabs["csk"], tabs["cosq"], tabs["sinq"])


def _dilated_kernel(q_ref, kc_ref, kp_ref, vc_ref, vp_ref, bias_ref, o_ref, lse_ref,
                    kk_ref, vlo_ref, vhi_ref, qlo_ref, qhi_ref, *, nsub):
    blk = pl.program_id(2)
    lane = lax.broadcasted_iota(jnp.int32, (1, A_WIDTH), 1) % LANES
    lo = lane < A_HEAD_DIM
    q = q_ref[...]
    qlo_ref[...] = jnp.where(lo, q, jnp.zeros_like(q))
    qhi_ref[...] = jnp.where(lo, jnp.zeros_like(q), q)
    kk_ref[0:BAND, :] = kp_ref[...]
    kk_ref[BAND:, :] = kc_ref[...]
    vp, vc = vp_ref[...], vc_ref[...]
    vlo_ref[0:BAND, :] = jnp.where(lo, vp, jnp.zeros_like(vp))
    vlo_ref[BAND:, :] = jnp.where(lo, vc, jnp.zeros_like(vc))
    vhi_ref[0:BAND, :] = jnp.where(lo, jnp.zeros_like(vp), vp)
    vhi_ref[BAND:, :] = jnp.where(lo, jnp.zeros_like(vc), vc)
    lane_t = lax.broadcasted_iota(jnp.int32, (BAND, LANES), 1)
    lo_t = lane_t < A_HEAD_DIM

    def sub(c, carry):
        r0 = pl.multiple_of(c * BAND, BAND)
        first = jnp.logical_and(blk == 0, c == 0).astype(jnp.int32)
        lse_tile = jnp.zeros((BAND, LANES), F32)
        for j in range(HEAD_PAIRS):
            cols = slice(j * LANES, (j + 1) * LANES)
            keys = kk_ref[pl.ds(r0, 2 * BAND), cols]
            ps, rls = [], []
            for hh, qsrc in enumerate((qlo_ref, qhi_ref)):
                h = 2 * j + hh
                s = _dot_nt(qsrc[pl.ds(r0, BAND), cols], keys) + bias_ref[first, h]
                m = jnp.max(s, axis=1, keepdims=True)
                p = jnp.exp(s - m)
                l = jnp.sum(p, axis=1, keepdims=True)
                ps.append(p.astype(BF16))
                rls.append(1.0 / l)
                lse_tile = jnp.where(lane_t == h, m + jnp.log(l), lse_tile)
            pv = _dot(ps[0], vlo_ref[pl.ds(r0, 2 * BAND), cols]) + _dot(ps[1], vhi_ref[pl.ds(r0, 2 * BAND), cols])
            o_ref[pl.ds(r0, BAND), cols] = (pv * jnp.where(lo_t, rls[0], rls[1])).astype(BF16)
        lse_ref[pl.ds(r0, BAND), :] = lse_tile
        return carry

    lax.fori_loop(0, nsub, sub, 0)


def _dilated_bias(dilation):
    qi = np.arange(BAND)[:, None]
    kj = np.arange(2 * BAND)[None, :]
    rel = qi + BAND - kj
    valid = (rel >= 0) & (rel <= BAND)
    slopes = 2.0 ** (-8.0 * np.arange(1, A_HEADS + 1, dtype=np.float64) / A_HEADS)
    bias = -slopes[:, None, None] * (rel * dilation).astype(np.float64)[None]
    gen = np.where(valid[None], bias, NEG)
    fst = np.where((valid & (kj >= BAND))[None], bias, NEG)
    return jnp.asarray(np.stack([gen, fst]), F32)


def _dilated(aqkv, dilation, tq):
    bsz, seq, _ = aqkv.shape
    n = seq // dilation
    nsub = tq // BAND
    view = aqkv.reshape(bsz, n, dilation * 3 * A_WIDTH)
    cur = lambda which: pl.BlockSpec((None, tq, A_WIDTH), lambda b, r, i: (b, i, 3 * r + which))
    prev = lambda which: pl.BlockSpec((None, BAND, A_WIDTH),
                                      lambda b, r, i: (b, jnp.maximum(i * nsub - 1, 0), 3 * r + which))
    bias = _dilated_bias(dilation)
    o, lse = pl.pallas_call(
        functools.partial(_dilated_kernel, nsub=nsub),
        grid=(bsz, dilation, n // tq),
        in_specs=[cur(0), cur(1), prev(1), cur(2), prev(2),
                  pl.BlockSpec(bias.shape, lambda b, r, i: (0, 0, 0, 0))],
        out_specs=[pl.BlockSpec((None, tq, A_WIDTH), lambda b, r, i: (b, i, r)),
                   pl.BlockSpec((None, tq, LANES), lambda b, r, i: (b, i, r))],
        out_shape=[jax.ShapeDtypeStruct((bsz, n, dilation * A_WIDTH), BF16),
                   jax.ShapeDtypeStruct((bsz, n, dilation * LANES), F32)],
        scratch_shapes=[pltpu.VMEM((tq + BAND, A_WIDTH), BF16)] * 3 + [pltpu.VMEM((tq, A_WIDTH), BF16)] * 2,
        compiler_params=pltpu.CompilerParams(dimension_semantics=("arbitrary",) * 3,
                                             vmem_limit_bytes=VMEM_LIMIT),
        name=f"dilated_d{dilation}",
    )(view, view, view, view, view, bias)
    return o.reshape(bsz, seq, A_WIDTH), lse.reshape(bsz, seq, LANES)


def _mla_kernel(qi_ref, kj_ref, q_ref, k_ref, v_ref, g_ref, o_ref, m_ref, l_ref, acc_ref, *, tq):
    t = pl.program_id(2)
    qi, kj = qi_ref[t], kj_ref[t]

    @pl.when(kj == 0)
    def _():
        m_ref[...] = jnp.full(m_ref.shape, NEG, F32)
        l_ref[...] = jnp.zeros(l_ref.shape, F32)
        acc_ref[...] = jnp.zeros(acc_ref.shape, F32)

    lane = lax.broadcasted_iota(jnp.int32, (1, LANES), 1)
    lo = lane < B_V_DIM

    def step(masked):
        v = v_ref[...]
        vsel = (jnp.where(lo, v, jnp.zeros_like(v)), jnp.where(lo, jnp.zeros_like(v), v))
        pv = None
        alphas = []
        for hh in range(2):
            cols = slice(hh * LANES, (hh + 1) * LANES)
            s = _dot_nt(q_ref[:, cols], k_ref[:, cols])
            if masked:
                row = lax.broadcasted_iota(jnp.int32, s.shape, 0)
                col = lax.broadcasted_iota(jnp.int32, s.shape, 1)
                s = jnp.where(col <= row, s, NEG)
            m_old = m_ref[hh]
            m_new = jnp.maximum(m_old, jnp.max(s, axis=1, keepdims=True))
            alpha = jnp.exp(m_old - m_new)
            p = jnp.exp(s - m_new)
            l_ref[hh] = alpha * l_ref[hh] + jnp.sum(p, axis=1, keepdims=True)
            m_ref[hh] = m_new
            alphas.append(alpha)
            d = _dot(p.astype(BF16), vsel[hh])
            pv = d if pv is None else pv + d
        acc_ref[...] = acc_ref[...] * jnp.where(lo, alphas[0], alphas[1]) + pv

    @pl.when(kj < qi)
    def _():
        step(False)

    @pl.when(kj == qi)
    def _():
        step(True)
        inv = jnp.where(lo, 1.0 / l_ref[0], 1.0 / l_ref[1])
        o_ref[...] = (acc_ref[...] * inv * g_ref[...].astype(F32)).astype(BF16)


def _mla(qf, kf, v, gates, tq):
    bsz, seq, _ = qf.shape
    nq = seq // tq
    qi = np.concatenate([np.full(i + 1, i) for i in range(nq)]).astype(np.int32)
    kj = np.concatenate([np.arange(i + 1) for i in range(nq)]).astype(np.int32)
    grid_spec = pltpu.PrefetchScalarGridSpec(
        num_scalar_prefetch=2,
        grid=(bsz, HEAD_PAIRS, len(qi)),
        in_specs=[pl.BlockSpec((None, tq, 2 * LANES), lambda b, p, t, qi, kj: (b, qi[t], p)),
                  pl.BlockSpec((None, tq, 2 * LANES), lambda b, p, t, qi, kj: (b, kj[t], p)),
                  pl.BlockSpec((None, tq, LANES), lambda b, p, t, qi, kj: (b, kj[t], p)),
                  pl.BlockSpec((None, tq, LANES), lambda b, p, t, qi, kj: (b, qi[t], HEAD_PAIRS + p))],
        out_specs=pl.BlockSpec((None, tq, LANES), lambda b, p, t, qi, kj: (b, qi[t], p)),
        scratch_shapes=[pltpu.VMEM((2, tq, 1), F32), pltpu.VMEM((2, tq, 1), F32), pltpu.VMEM((tq, LANES), F32)],
    )
    return pl.pallas_call(
        functools.partial(_mla_kernel, tq=tq),
        grid_spec=grid_spec,
        out_shape=jax.ShapeDtypeStruct((bsz, seq, B_WIDTH), BF16),
        compiler_params=pltpu.CompilerParams(dimension_semantics=("arbitrary",) * 3,
                                             vmem_limit_bytes=VMEM_LIMIT),
        name="mla",
    )(jnp.asarray(qi), jnp.asarray(kj), qf, kf, v, gates)


def _out_kernel(o1_ref, o2_ref, o3_ref, l1_ref, l2_ref, l3_ref, gate_ref, b_ref, x_ref, wo_ref, expand_ref,
                lng_ref, lnb_ref, out_ref):
    l1, l2, l3 = l1_ref[...], l2_ref[...], l3_ref[...]
    m = jnp.maximum(jnp.maximum(l1, l2), l3)
    e1, e2, e3 = jnp.exp(l1 - m), jnp.exp(l2 - m), jnp.exp(l3 - m)
    rden = 1.0 / (e1 + e2 + e3)
    ex = lambda w: _dot(w.astype(BF16), expand_ref[...])
    a = (ex(e1 * rden) * o1_ref[...].astype(F32) + ex(e2 * rden) * o2_ref[...].astype(F32)
         + ex(e3 * rden) * o3_ref[...].astype(F32))
    a = a * gate_ref[...].astype(F32)
    y = _dot(a.astype(BF16), wo_ref[0:A_WIDTH, :]) + _dot(b_ref[...], wo_ref[A_WIDTH:, :])
    z = DEEPNORM_ALPHA * x_ref[...] + y
    mu = jnp.mean(z, axis=-1, keepdims=True)
    zc = z - mu
    var = jnp.mean(zc * zc, axis=-1, keepdims=True)
    out_ref[...] = zc * lax.rsqrt(var + LN_EPS) * lng_ref[...] + lnb_ref[...]


def _finish(os_, lses, gates, b_out, x, wo, expand, ln_g, ln_b, tm):
    bsz, seq, _ = x.shape
    row = lambda w: pl.BlockSpec((None, tm, w), lambda b, i: (b, i, 0))
    full = lambda a: pl.BlockSpec(a.shape, lambda b, i: (0,) * a.ndim)
    return pl.pallas_call(
        _out_kernel,
        grid=(bsz, seq // tm),
        in_specs=[row(A_WIDTH)] * 3 + [row(LANES)] * 3 + [row(A_WIDTH), row(B_WIDTH), row(D_MODEL),
                                                         full(wo), full(expand), full(ln_g), full(ln_b)],
        out_specs=row(D_MODEL),
        out_shape=jax.ShapeDtypeStruct((bsz, seq, D_MODEL), F32),
        compiler_params=pltpu.CompilerParams(dimension_semantics=("arbitrary", "arbitrary"),
                                             vmem_limit_bytes=VMEM_LIMIT),
        name="merge_out_ln",
    )(*os_, *lses, gates, b_out, x, wo, expand, ln_g, ln_b)


def _rot_cols(w):
    half = w.shape[-1] // 2
    return jnp.concatenate([-w[..., half:], w[..., :half]], axis=-1)


def _layout_weights(w_in, q_norm_g, w_uq, kv_norm_g, w_ukv, w_o):
    aw = A_WIDTH
    a_q, a_k, a_v, a_gate = (w_in[:, i * aw:(i + 1) * aw] for i in range(4))
    o = 4 * aw
    c_q, o = w_in[:, o:o + Q_LORA_RANK], o + Q_LORA_RANK
    c_kv, o = w_in[:, o:o + KV_LORA_RANK], o + KV_LORA_RANK
    k_rope, o = w_in[:, o:o + B_ROPE_DIM], o + B_ROPE_DIM
    b_gate = w_in[:, o:o + B_WIDTH]
    pad = jnp.zeros((D_MODEL, LANES - 2 * B_ROPE_DIM), w_in.dtype)
    win = jnp.concatenate([a_q * (A_HEAD_DIM ** -0.5), a_k, a_v, a_gate, b_gate, c_q, c_kv,
                           k_rope, _rot_cols(k_rope), pad], axis=1).astype(BF16)
    uq = w_uq.reshape(Q_LORA_RANK, B_HEADS, B_NOPE_DIM + B_ROPE_DIM)
    nope, rope = uq[..., :B_NOPE_DIM], uq[..., B_NOPE_DIM:]
    z = lambda w: jnp.zeros((Q_LORA_RANK, B_HEADS, w), w_uq.dtype)
    tail = LANES - B_NOPE_DIM - B_ROPE_DIM
    w1 = jnp.concatenate([nope, rope, z(tail)], axis=-1).reshape(Q_LORA_RANK, B_HEADS * LANES)
    w2 = jnp.concatenate([z(B_NOPE_DIM), _rot_cols(rope), z(tail)], axis=-1).reshape(Q_LORA_RANK, B_HEADS * LANES)
    wq = jnp.concatenate([w1, w2], axis=1).astype(BF16)
    ukv = w_ukv.reshape(KV_LORA_RANK, B_HEADS, B_NOPE_DIM + B_V_DIM)
    wk = jnp.concatenate([ukv[..., :B_NOPE_DIM], jnp.zeros((KV_LORA_RANK, B_HEADS, LANES - B_NOPE_DIM), w_ukv.dtype)],
                         axis=-1).reshape(KV_LORA_RANK, B_HEADS * LANES)
    wv = ukv[..., B_NOPE_DIM:].reshape(KV_LORA_RANK, B_WIDTH)
    wkv = jnp.concatenate([wk, wv], axis=1).astype(BF16)
    place = np.zeros((LANES, B_HEADS * LANES), np.float32)
    for h in range(B_HEADS):
        place[np.arange(B_ROPE_DIM), h * LANES + B_NOPE_DIM + np.arange(B_ROPE_DIM)] = 1.0
    expand = np.zeros((LANES, A_WIDTH), np.float32)
    for h in range(A_HEADS):
        expand[h, h * A_HEAD_DIM:(h + 1) * A_HEAD_DIM] = 1.0
    return {"win": win, "wq": wq, "wkv": wkv, "place": jnp.asarray(place, BF16),
            "gq": q_norm_g.reshape(1, -1).astype(F32), "gkv": kv_norm_g.reshape(1, -1).astype(F32),
            "wo": w_o.astype(BF16), "expand": jnp.asarray(expand, BF16)}


def _rope_tables(seq):
    inv_freq = ROPE_THETA ** (-jnp.arange(0, B_ROPE_DIM, 2, dtype=F32) / B_ROPE_DIM)
    ang = jnp.arange(seq, dtype=F32)[:, None] * inv_freq[None, :]
    ang = jnp.concatenate([ang, ang], axis=-1)
    cos, sin = jnp.cos(ang), jnp.sin(ang)
    scale = (B_NOPE_DIM + B_ROPE_DIM) ** -0.5
    one = jnp.ones((seq, B_NOPE_DIM), F32)
    zn = jnp.zeros((seq, B_NOPE_DIM), F32)
    zt = jnp.zeros((seq, LANES - B_NOPE_DIM - B_ROPE_DIM), F32)
    cosq = jnp.tile(jnp.concatenate([one, cos, zt], axis=1) * scale, (1, B_HEADS))
    sinq = jnp.tile(jnp.concatenate([zn, sin, zt], axis=1) * scale, (1, B_HEADS))
    csk = jnp.concatenate([cos, sin, jnp.zeros((seq, LANES - 2 * B_ROPE_DIM), F32)], axis=1)
    return {"csk": csk, "cosq": cosq, "sinq": sinq}


def kernel(x, w_in, q_norm_g, w_uq, kv_norm_g, w_ukv, w_o, ln_g, ln_b):
    bsz, seq, _ = x.shape
    wts = _layout_weights(w_in, q_norm_g, w_uq, kv_norm_g, w_ukv, w_o)
    tabs = _rope_tables(seq)
    aqkv, gates, qf, kf, v = _project(x, wts, tabs, tm=512)
    os_, lses = [], []
    for _, dilation in DILATED_PATTERNS:
        o, lse = _dilated(aqkv, dilation, tq=min(512, seq // dilation))
        os_.append(o)
        lses.append(lse)
    b_out = _mla(qf, kf, v, gates, tq=512)
    return _finish(os_, lses, gates, b_out, x, wts["wo"], wts["expand"],
                   ln_g.reshape(1, -1).astype(F32), ln_b.reshape(1, -1).astype(F32), tm=512)
```

```python
import functools

import jax
import jax.numpy as jnp
import numpy as np
from jax import lax
from jax.experimental import pallas as pl
from jax.experimental.pallas import tpu as pltpu

D_MODEL = 1024
A_HEADS = 8
A_HEAD_DIM = 64
A_WIDTH = A_HEADS * A_HEAD_DIM
DILATED_PATTERNS = ((128, 1), (512, 4), (2048, 16))
BAND = 128
B_HEADS = 8
B_NOPE_DIM = 64
B_ROPE_DIM = 32
B_V_DIM = 64
B_WIDTH = B_HEADS * B_V_DIM
Q_LORA_RANK = 256
KV_LORA_RANK = 128
ROPE_THETA = 10000.0
LN_EPS = 1e-5
RMS_EPS = 1e-6
DEPTH = 1
DEEPNORM_ALPHA = (2 * DEPTH) ** 0.25

LANES = 128
HEAD_PAIRS = A_HEADS // 2
NEG = -1e30
VMEM_LIMIT = 56 * 1024 * 1024

BF16 = jnp.bfloat16
F32 = jnp.float32

C_AQKV = 0
C_GATE = 1536
C_CQ = 2560
C_CKV = 2816
C_KR = 2944
IN_COLS = 3072


def _dot(a, b):
    return jnp.dot(a, b, preferred_element_type=F32)


def _dot_nt(a, b):
    return lax.dot_general(a, b, (((1,), (1,)), ((), ())), preferred_element_type=F32)


def _rms(t, g):
    return t * lax.rsqrt(jnp.mean(t * t, axis=-1, keepdims=True) + RMS_EPS) * g


def _silu(t):
    return t / (1.0 + jnp.exp(-t))


def _proj_kernel(x_ref, win_ref, wq_ref, wk_ref, wvt_ref, place_ref, gq_ref, gkv_ref, csk_ref, cosq_ref, sinq_ref,
                 aqkv_ref, gate_ref, qf_ref, kf_ref, vt_ref):
    xb = x_ref[...].astype(BF16)
    aqkv_ref[...] = _dot(xb, win_ref[:, C_AQKV:C_GATE]).astype(BF16)
    gate_ref[...] = _silu(_dot(xb, win_ref[:, C_GATE:C_CQ])).astype(BF16)
    lat = _dot(xb, win_ref[:, C_CQ:IN_COLS])
    cq = lat[:, 0:Q_LORA_RANK]
    ckv = lat[:, Q_LORA_RANK:Q_LORA_RANK + KV_LORA_RANK]
    rope = lat[:, Q_LORA_RANK + KV_LORA_RANK:]
    q2 = _dot(_rms(cq, gq_ref[...]).astype(BF16), wq_ref[...])
    qf_ref[...] = (q2[:, :B_HEADS * LANES] * cosq_ref[...] + q2[:, B_HEADS * LANES:] * sinq_ref[...]).astype(BF16)
    r2 = rope * csk_ref[...]
    kr = r2 + pltpu.roll(r2, LANES - B_ROPE_DIM, axis=1)
    kvn = _rms(ckv, gkv_ref[...]).astype(BF16)
    kf_ref[...] = (_dot(kvn, wk_ref[...]) + _dot(kr.astype(BF16), place_ref[...])).astype(BF16)
    vt_ref[...] = _dot_nt(wvt_ref[...], kvn).astype(BF16)


def _project(x, wts, tabs, tm):
    bsz, seq, _ = x.shape
    row = lambda w: pl.BlockSpec((None, tm, w), lambda i, b: (b, i, 0))
    full = lambda a: pl.BlockSpec(a.shape, lambda i, b: (0,) * a.ndim)
    tab = lambda w: pl.BlockSpec((tm, w), lambda i, b: (i, 0))
    out = lambda w: jax.ShapeDtypeStruct((bsz, seq, w), BF16)
    return pl.pallas_call(
        _proj_kernel,
        grid=(seq // tm, bsz),
        in_specs=[row(D_MODEL), full(wts["win"]), full(wts["wq"]), full(wts["wk"]), full(wts["wvt"]),
                  full(wts["place"]), full(wts["gq"]), full(wts["gkv"]),
                  tab(LANES), tab(B_HEADS * LANES), tab(B_HEADS * LANES)],
        out_specs=[row(3 * A_WIDTH), row(A_WIDTH + B_WIDTH), row(B_HEADS * LANES), row(B_HEADS * LANES),
                   pl.BlockSpec((None, B_WIDTH, tm), lambda i, b: (b, 0, i))],
        out_shape=[out(3 * A_WIDTH), out(A_WIDTH + B_WIDTH), out(B_HEADS * LANES), out(B_HEADS * LANES),
                   jax.ShapeDtypeStruct((bsz, B_WIDTH, seq), BF16)],
        compiler_params=pltpu.CompilerParams(dimension_semantics=("arbitrary", "arbitrary"),
                                             vmem_limit_bytes=VMEM_LIMIT),
        name="proj",
    )(x, wts["win"], wts["wq"], wts["wk"], wts["wvt"], wts["place"], wts["gq"], wts["gkv"],
      tabs["csk"], tabs["cosq"], tabs["sinq"])


def _dilated_kernel(q_ref, kc_ref, kp_ref, vc_ref, vp_ref, bias_ref, o_ref, lse_ref,
                    kk_ref, vlo_ref, vhi_ref, qlo_ref, qhi_ref, *, nsub):
    blk = pl.program_id(2)
    lane = lax.broadcasted_iota(jnp.int32, (1, A_WIDTH), 1) % LANES
    lo = lane < A_HEAD_DIM
    q = q_ref[...]
    qlo_ref[...] = jnp.where(lo, q, jnp.zeros_like(q))
    qhi_ref[...] = jnp.where(lo, jnp.zeros_like(q), q)
    kk_ref[0:BAND, :] = kp_ref[...]
    kk_ref[BAND:, :] = kc_ref[...]
    vp, vc = vp_ref[...], vc_ref[...]
    vlo_ref[0:BAND, :] = jnp.where(lo, vp, jnp.zeros_like(vp))
    vlo_ref[BAND:, :] = jnp.where(lo, vc, jnp.zeros_like(vc))
    vhi_ref[0:BAND, :] = jnp.where(lo, jnp.zeros_like(vp), vp)
    vhi_ref[BAND:, :] = jnp.where(lo, jnp.zeros_like(vc), vc)
    lane_t = lax.broadcasted_iota(jnp.int32, (BAND, LANES), 1)
    lo_t = lane_t < A_HEAD_DIM

    def sub(c, carry):
        r0 = pl.multiple_of(c * BAND, BAND)
        first = jnp.logical_and(blk == 0, c == 0).astype(jnp.int32)
        lse_tile = jnp.zeros((BAND, LANES), F32)
        for j in range(HEAD_PAIRS):
            cols = slice(j * LANES, (j + 1) * LANES)
            keys = kk_ref[pl.ds(r0, 2 * BAND), cols]
            ps, rls = [], []
            for hh, qsrc in enumerate((qlo_ref, qhi_ref)):
                h = 2 * j + hh
                s = _dot_nt(qsrc[pl.ds(r0, BAND), cols], keys) + bias_ref[first, h]
                m = jnp.max(s, axis=1, keepdims=True)
                p = jnp.exp(s - m)
                l = jnp.sum(p, axis=1, keepdims=True)
                ps.append(p.astype(BF16))
                rls.append(1.0 / l)
                lse_tile = jnp.where(lane_t == h, m + jnp.log(l), lse_tile)
            pv = _dot(ps[0], vlo_ref[pl.ds(r0, 2 * BAND), cols]) + _dot(ps[1], vhi_ref[pl.ds(r0, 2 * BAND), cols])
            o_ref[pl.ds(r0, BAND), cols] = (pv * jnp.where(lo_t, rls[0], rls[1])).astype(BF16)
        lse_ref[pl.ds(r0, BAND), :] = lse_tile
        return carry

    lax.fori_loop(0, nsub, sub, 0)


def _dilated_bias(dilation):
    qi = np.arange(BAND)[:, None]
    kj = np.arange(2 * BAND)[None, :]
    rel = qi + BAND - kj
    valid = (rel >= 0) & (rel <= BAND)
    slopes = 2.0 ** (-8.0 * np.arange(1, A_HEADS + 1, dtype=np.float64) / A_HEADS)
    bias = -slopes[:, None, None] * (rel * dilation).astype(np.float64)[None]
    gen = np.where(valid[None], bias, NEG)
    fst = np.where((valid & (kj >= BAND))[None], bias, NEG)
    return jnp.asarray(np.stack([gen, fst]), F32)


def _dilated(aqkv, dilation, tq):
    bsz, seq, _ = aqkv.shape
    n = seq // dilation
    nsub = tq // BAND
    view = aqkv.reshape(bsz, n, dilation * 3 * A_WIDTH)
    cur = lambda which: pl.BlockSpec((None, tq, A_WIDTH), lambda b, r, i: (b, i, 3 * r + which))
    prev = lambda which: pl.BlockSpec((None, BAND, A_WIDTH),
                                      lambda b, r, i: (b, jnp.maximum(i * nsub - 1, 0), 3 * r + which))
    bias = _dilated_bias(dilation)
    o, lse = pl.pallas_call(
        functools.partial(_dilated_kernel, nsub=nsub),
        grid=(bsz, dilation, n // tq),
        in_specs=[cur(0), cur(1), prev(1), cur(2), prev(2),
                  pl.BlockSpec(bias.shape, lambda b, r, i: (0, 0, 0, 0))],
        out_specs=[pl.BlockSpec((None, tq, A_WIDTH), lambda b, r, i: (b, i, r)),
                   pl.BlockSpec((None, tq, LANES), lambda b, r, i: (b, i, r))],
        out_shape=[jax.ShapeDtypeStruct((bsz, n, dilation * A_WIDTH), BF16),
                   jax.ShapeDtypeStruct((bsz, n, dilation * LANES), F32)],
        scratch_shapes=[pltpu.VMEM((tq + BAND, A_WIDTH), BF16)] * 3 + [pltpu.VMEM((tq, A_WIDTH), BF16)] * 2,
        compiler_params=pltpu.CompilerParams(dimension_semantics=("arbitrary",) * 3,
                                             vmem_limit_bytes=VMEM_LIMIT),
        name=f"dilated_d{dilation}",
    )(view, view, view, view, view, bias)
    return o.reshape(bsz, seq, A_WIDTH), lse.reshape(bsz, seq, LANES)


def _mla_kernel(qi_ref, kj_ref, q_ref, k_ref, vt_ref, g_ref, o_ref, m_ref, l_ref, acc_ref, *, npair):
    t = pl.program_id(2)
    qi, kj = qi_ref[t], kj_ref[t]

    @pl.when(kj == 0)
    def _():
        m_ref[...] = jnp.full(m_ref.shape, NEG, F32)
        l_ref[...] = jnp.zeros(l_ref.shape, F32)
        acc_ref[...] = jnp.zeros(acc_ref.shape, F32)

    def step(masked):
        for h in range(2 * npair):
            cols = slice(h * LANES, (h + 1) * LANES)
            rows = slice(h * B_V_DIM, (h + 1) * B_V_DIM)
            st = _dot_nt(k_ref[:, cols], q_ref[:, cols])
            if masked:
                kpos = lax.broadcasted_iota(jnp.int32, st.shape, 0)
                qpos = lax.broadcasted_iota(jnp.int32, st.shape, 1)
                st = jnp.where(kpos <= qpos, st, NEG)
            m_old = m_ref[h]
            m_new = jnp.maximum(m_old, jnp.max(st, axis=0, keepdims=True))
            alpha = jnp.exp(m_old - m_new)
            pt = jnp.exp(st - m_new)
            l_ref[h] = alpha * l_ref[h] + jnp.sum(pt, axis=0, keepdims=True)
            m_ref[h] = m_new
            acc_ref[rows, :] = alpha * acc_ref[rows, :] + _dot(vt_ref[rows, :], pt.astype(BF16))

    @pl.when(kj < qi)
    def _():
        step(False)

    @pl.when(kj == qi)
    def _():
        step(True)
        for h in range(2 * npair):
            rows = slice(h * B_V_DIM, (h + 1) * B_V_DIM)
            acc_ref[rows, :] = acc_ref[rows, :] * (1.0 / l_ref[h])
        o_ref[...] = (acc_ref[...].T * g_ref[...].astype(F32)).astype(BF16)


def _mla(qf, kf, vt, gates, tq, npair):
    bsz, seq, _ = qf.shape
    nq = seq // tq
    ngrp = HEAD_PAIRS // npair
    qi = np.concatenate([np.full(i + 1, i) for i in range(nq)]).astype(np.int32)
    kj = np.concatenate([np.arange(i + 1) for i in range(nq)]).astype(np.int32)
    wide, narrow = 2 * npair * LANES, npair * LANES
    grid_spec = pltpu.PrefetchScalarGridSpec(
        num_scalar_prefetch=2,
        grid=(bsz, ngrp, len(qi)),
        in_specs=[pl.BlockSpec((None, tq, wide), lambda b, p, t, qi, kj: (b, qi[t], p)),
                  pl.BlockSpec((None, tq, wide), lambda b, p, t, qi, kj: (b, kj[t], p)),
                  pl.BlockSpec((None, narrow, tq), lambda b, p, t, qi, kj: (b, p, kj[t])),
                  pl.BlockSpec((None, tq, narrow), lambda b, p, t, qi, kj: (b, qi[t], ngrp + p))],
        out_specs=pl.BlockSpec((None, tq, narrow), lambda b, p, t, qi, kj: (b, qi[t], p)),
        scratch_shapes=[pltpu.VMEM((2 * npair, 1, tq), F32), pltpu.VMEM((2 * npair, 1, tq), F32),
                        pltpu.VMEM((narrow, tq), F32)],
    )
    return pl.pallas_call(
        functools.partial(_mla_kernel, npair=npair),
        grid_spec=grid_spec,
        out_shape=jax.ShapeDtypeStruct((bsz, seq, B_WIDTH), BF16),
        compiler_params=pltpu.CompilerParams(dimension_semantics=("arbitrary",) * 3,
                                             vmem_limit_bytes=VMEM_LIMIT),
        name="mla",
    )(jnp.asarray(qi), jnp.asarray(kj), qf, kf, vt, gates)


def _out_kernel(o1_ref, o2_ref, o3_ref, l1_ref, l2_ref, l3_ref, gate_ref, b_ref, x_ref, wo_ref, expand_ref,
                lng_ref, lnb_ref, out_ref):
    l1, l2, l3 = l1_ref[...], l2_ref[...], l3_ref[...]
    m = jnp.maximum(jnp.maximum(l1, l2), l3)
    e1, e2, e3 = jnp.exp(l1 - m), jnp.exp(l2 - m), jnp.exp(l3 - m)
    rden = 1.0 / (e1 + e2 + e3)
    ex = lambda w: _dot(w.astype(BF16), expand_ref[...])
    a = (ex(e1 * rden) * o1_ref[...].astype(F32) + ex(e2 * rden) * o2_ref[...].astype(F32)
         + ex(e3 * rden) * o3_ref[...].astype(F32))
    a = a * gate_ref[...].astype(F32)
    y = _dot(a.astype(BF16), wo_ref[0:A_WIDTH, :]) + _dot(b_ref[...], wo_ref[A_WIDTH:, :])
    z = DEEPNORM_ALPHA * x_ref[...] + y
    mu = jnp.mean(z, axis=-1, keepdims=True)
    zc = z - mu
    var = jnp.mean(zc * zc, axis=-1, keepdims=True)
    out_ref[...] = zc * lax.rsqrt(var + LN_EPS) * lng_ref[...] + lnb_ref[...]


def _finish(os_, lses, gates, b_out, x, wo, expand, ln_g, ln_b, tm):
    bsz, seq, _ = x.shape
    row = lambda w: pl.BlockSpec((None, tm, w), lambda b, i: (b, i, 0))
    full = lambda a: pl.BlockSpec(a.shape, lambda b, i: (0,) * a.ndim)
    return pl.pallas_call(
        _out_kernel,
        grid=(bsz, seq // tm),
        in_specs=[row(A_WIDTH)] * 3 + [row(LANES)] * 3 + [row(A_WIDTH), row(B_WIDTH), row(D_MODEL),
                                                         full(wo), full(expand), full(ln_g), full(ln_b)],
        out_specs=row(D_MODEL),
        out_shape=jax.ShapeDtypeStruct((bsz, seq, D_MODEL), F32),
        compiler_params=pltpu.CompilerParams(dimension_semantics=("arbitrary", "arbitrary"),
                                             vmem_limit_bytes=VMEM_LIMIT),
        name="merge_out_ln",
    )(*os_, *lses, gates, b_out, x, wo, expand, ln_g, ln_b)


def _rot_cols(w):
    half = w.shape[-1] // 2
    return jnp.concatenate([-w[..., half:], w[..., :half]], axis=-1)


def _layout_weights(w_in, q_norm_g, w_uq, kv_norm_g, w_ukv, w_o):
    aw = A_WIDTH
    a_q, a_k, a_v, a_gate = (w_in[:, i * aw:(i + 1) * aw] for i in range(4))
    o = 4 * aw
    c_q, o = w_in[:, o:o + Q_LORA_RANK], o + Q_LORA_RANK
    c_kv, o = w_in[:, o:o + KV_LORA_RANK], o + KV_LORA_RANK
    k_rope, o = w_in[:, o:o + B_ROPE_DIM], o + B_ROPE_DIM
    b_gate = w_in[:, o:o + B_WIDTH]
    pad = jnp.zeros((D_MODEL, LANES - 2 * B_ROPE_DIM), w_in.dtype)
    win = jnp.concatenate([a_q * (A_HEAD_DIM ** -0.5), a_k, a_v, a_gate, b_gate, c_q, c_kv,
                           k_rope, _rot_cols(k_rope), pad], axis=1).astype(BF16)
    uq = w_uq.reshape(Q_LORA_RANK, B_HEADS, B_NOPE_DIM + B_ROPE_DIM)
    nope, rope = uq[..., :B_NOPE_DIM], uq[..., B_NOPE_DIM:]
    z = lambda w: jnp.zeros((Q_LORA_RANK, B_HEADS, w), w_uq.dtype)
    tail = LANES - B_NOPE_DIM - B_ROPE_DIM
    w1 = jnp.concatenate([nope, rope, z(tail)], axis=-1).reshape(Q_LORA_RANK, B_HEADS * LANES)
    w2 = jnp.concatenate([z(B_NOPE_DIM), _rot_cols(rope), z(tail)], axis=-1).reshape(Q_LORA_RANK, B_HEADS * LANES)
    wq = jnp.concatenate([w1, w2], axis=1).astype(BF16)
    ukv = w_ukv.reshape(KV_LORA_RANK, B_HEADS, B_NOPE_DIM + B_V_DIM)
    wk = jnp.concatenate([ukv[..., :B_NOPE_DIM], jnp.zeros((KV_LORA_RANK, B_HEADS, LANES - B_NOPE_DIM), w_ukv.dtype)],
                         axis=-1).reshape(KV_LORA_RANK, B_HEADS * LANES)
    wvt = ukv[..., B_NOPE_DIM:].reshape(KV_LORA_RANK, B_WIDTH).T
    place = np.zeros((LANES, B_HEADS * LANES), np.float32)
    for h in range(B_HEADS):
        place[np.arange(B_ROPE_DIM), h * LANES + B_NOPE_DIM + np.arange(B_ROPE_DIM)] = 1.0
    expand = np.zeros((LANES, A_WIDTH), np.float32)
    for h in range(A_HEADS):
        expand[h, h * A_HEAD_DIM:(h + 1) * A_HEAD_DIM] = 1.0
    return {"win": win, "wq": wq, "wk": wk.astype(BF16), "wvt": wvt.astype(BF16), "place": jnp.asarray(place, BF16),
            "gq": q_norm_g.reshape(1, -1).astype(F32), "gkv": kv_norm_g.reshape(1, -1).astype(F32),
            "wo": w_o.astype(BF16), "expand": jnp.asarray(expand, BF16)}


def _rope_tables(seq):
    inv_freq = ROPE_THETA ** (-jnp.arange(0, B_ROPE_DIM, 2, dtype=F32) / B_ROPE_DIM)
    ang = jnp.arange(seq, dtype=F32)[:, None] * inv_freq[None, :]
    ang = jnp.concatenate([ang, ang], axis=-1)
    cos, sin = jnp.cos(ang), jnp.sin(ang)
    scale = (B_NOPE_DIM + B_ROPE_DIM) ** -0.5
    one = jnp.ones((seq, B_NOPE_DIM), F32)
    zn = jnp.zeros((seq, B_NOPE_DIM), F32)
    zt = jnp.zeros((seq, LANES - B_NOPE_DIM - B_ROPE_DIM), F32)
    cosq = jnp.tile(jnp.concatenate([one, cos, zt], axis=1) * scale, (1, B_HEADS))
    sinq = jnp.tile(jnp.concatenate([zn, sin, zt], axis=1) * scale, (1, B_HEADS))
    csk = jnp.concatenate([cos, sin, jnp.zeros((seq, LANES - 2 * B_ROPE_DIM), F32)], axis=1)
    return {"csk": csk, "cosq": cosq, "sinq": sinq}


def kernel(x, w_in, q_norm_g, w_uq, kv_norm_g, w_ukv, w_o, ln_g, ln_b):
    bsz, seq, _ = x.shape
    wts = _layout_weights(w_in, q_norm_g, w_uq, kv_norm_g, w_ukv, w_o)
    tabs = _rope_tables(seq)
    aqkv, gates, qf, kf, vt = _project(x, wts, tabs, tm=512)
    os_, lses = [], []
    for _, dilation in DILATED_PATTERNS:
        o, lse = _dilated(aqkv, dilation, tq=min(512, seq // dilation))
        os_.append(o)
        lses.append(lse)
    b_out = _mla(qf, kf, vt, gates, tq=512, npair=2)
    return _finish(os_, lses, gates, b_out, x, wts["wo"], wts["expand"],
                   ln_g.reshape(1, -1).astype(F32), ln_b.reshape(1, -1).astype(F32), tm=512)
```

```python
import functools

import jax
import jax.numpy as jnp
import numpy as np
from jax import lax
from jax.experimental import pallas as pl
from jax.experimental.pallas import tpu as pltpu

D_MODEL = 1024
A_HEADS = 8
A_HEAD_DIM = 64
A_WIDTH = A_HEADS * A_HEAD_DIM
DILATED_PATTERNS = ((128, 1), (512, 4), (2048, 16))
DILATIONS = tuple(d for _, d in DILATED_PATTERNS)
BAND = 128
B_HEADS = 8
B_NOPE_DIM = 64
B_ROPE_DIM = 32
B_V_DIM = 64
B_WIDTH = B_HEADS * B_V_DIM
Q_LORA_RANK = 256
KV_LORA_RANK = 128
ROPE_THETA = 10000.0
LN_EPS = 1e-5
RMS_EPS = 1e-6
DEPTH = 1
DEEPNORM_ALPHA = (2 * DEPTH) ** 0.25

LANES = 128
SUBLANES = 8
HEAD_PAIRS = A_HEADS // 2
NEG = -1e30
VMEM_LIMIT = 56 * 1024 * 1024

BF16 = jnp.bfloat16
F32 = jnp.float32

C_AQKV = 0
C_GATE = 1536
C_CQ = 2560
C_CKV = 2816
C_KR = 2944
IN_COLS = 3072
AQKV_SLABS = 3 * A_WIDTH // LANES

TN_DIMS = (((0,), (0,)), ((), ()))
DILATED_GROUP = 4


def _dot(a, b):
    return jnp.dot(a, b, preferred_element_type=F32)


def _dot_nt(a, b):
    return lax.dot_general(a, b, (((1,), (1,)), ((), ())), preferred_element_type=F32)


def _rms(t, g):
    return t * lax.rsqrt(jnp.mean(t * t, axis=-1, keepdims=True) + RMS_EPS) * g


def _silu(t):
    return t / (1.0 + jnp.exp(-t))


def _proj_kernel(x_ref, win_ref, wq_ref, wk_ref, wvt_ref, place_ref, gq_ref, gkv_ref, csk_ref, cosq_ref, sinq_ref,
                 aqkv1_ref, aqkv4_ref, aqkv16_ref, gate_ref, qf_ref, kf_ref, vt_ref, slab_ref):
    xb = x_ref[...].astype(BF16)
    tm = xb.shape[0]
    aqkv = _dot(xb, win_ref[:, C_AQKV:C_GATE])
    aqkv1_ref[...] = aqkv.astype(BF16)
    for sl in range(AQKV_SLABS):
        slab_ref[sl] = aqkv[:, sl * LANES:(sl + 1) * LANES]
    for d, ref in ((DILATIONS[1], aqkv4_ref), (DILATIONS[2], aqkv16_ref)):
        for r in range(d):
            for sl in range(AQKV_SLABS):
                ref[r, :, sl * LANES:(sl + 1) * LANES] = slab_ref[sl, pl.ds(r, tm // d, stride=d), :].astype(BF16)
    gate_ref[...] = _silu(_dot(xb, win_ref[:, C_GATE:C_CQ])).astype(BF16)
    lat = _dot(xb, win_ref[:, C_CQ:IN_COLS])
    cq = lat[:, 0:Q_LORA_RANK]
    ckv = lat[:, Q_LORA_RANK:Q_LORA_RANK + KV_LORA_RANK]
    rope = lat[:, Q_LORA_RANK + KV_LORA_RANK:]
    q2 = _dot(_rms(cq, gq_ref[...]).astype(BF16), wq_ref[...])
    qf_ref[...] = (q2[:, :B_HEADS * LANES] * cosq_ref[...] + q2[:, B_HEADS * LANES:] * sinq_ref[...]).astype(BF16)
    r2 = rope * csk_ref[...]
    kr = r2 + pltpu.roll(r2, LANES - B_ROPE_DIM, axis=1)
    kvn = _rms(ckv, gkv_ref[...]).astype(BF16)
    kf_ref[...] = (_dot(kvn, wk_ref[...]) + _dot(kr.astype(BF16), place_ref[...])).astype(BF16)
    vt_ref[...] = _dot_nt(wvt_ref[...], kvn).astype(BF16)


def _project(x, wts, tabs, tm):
    bsz, seq, _ = x.shape
    row = lambda w: pl.BlockSpec((None, tm, w), lambda i, b: (b, i, 0))
    res = lambda d: pl.BlockSpec((None, d, tm // d, 3 * A_WIDTH), lambda i, b: (b, 0, i, 0))
    full = lambda a: pl.BlockSpec(a.shape, lambda i, b: (0,) * a.ndim)
    tab = lambda w: pl.BlockSpec((tm, w), lambda i, b: (i, 0))
    out = lambda w: jax.ShapeDtypeStruct((bsz, seq, w), BF16)
    res_out = lambda d: jax.ShapeDtypeStruct((bsz, d, seq // d, 3 * A_WIDTH), BF16)
    return pl.pallas_call(
        _proj_kernel,
        grid=(seq // tm, bsz),
        in_specs=[row(D_MODEL), full(wts["win"]), full(wts["wq"]), full(wts["wk"]), full(wts["wvt"]),
                  full(wts["place"]), full(wts["gq"]), full(wts["gkv"]),
                  tab(LANES), tab(B_HEADS * LANES), tab(B_HEADS * LANES)],
        out_specs=[row(3 * A_WIDTH), res(DILATIONS[1]), res(DILATIONS[2]), row(A_WIDTH + B_WIDTH),
                   row(B_HEADS * LANES), row(B_HEADS * LANES),
                   pl.BlockSpec((None, B_WIDTH, tm), lambda i, b: (b, 0, i))],
        out_shape=[out(3 * A_WIDTH), res_out(DILATIONS[1]), res_out(DILATIONS[2]), out(A_WIDTH + B_WIDTH),
                   out(B_HEADS * LANES), out(B_HEADS * LANES),
                   jax.ShapeDtypeStruct((bsz, B_WIDTH, seq), BF16)],
        scratch_shapes=[pltpu.VMEM((AQKV_SLABS, tm, LANES), F32)],
        compiler_params=pltpu.CompilerParams(dimension_semantics=("arbitrary", "arbitrary"),
                                             vmem_limit_bytes=VMEM_LIMIT),
        name="proj",
    )(x, wts["win"], wts["wq"], wts["wk"], wts["wvt"], wts["place"], wts["gq"], wts["gkv"],
      tabs["csk"], tabs["cosq"], tabs["sinq"])


def _dilated_kernel(q_ref, kc_ref, kp_ref, vc_ref, vp_ref, bias_ref, o_ref, lse_ref,
                    kk_ref, vt_ref, qq_ref, *, nsub, group):
    blk = pl.program_id(2)
    lane = lax.broadcasted_iota(jnp.int32, (1, A_WIDTH), 1) % LANES
    lo = lane < A_HEAD_DIM
    for c in range(nsub):
        q = q_ref[c * BAND:(c + 1) * BAND, :]
        qq_ref[c, 0:BAND, :] = jnp.where(lo, q, jnp.zeros_like(q))
        qq_ref[c, BAND:, :] = jnp.where(lo, jnp.zeros_like(q), q)
    kk_ref[0:BAND, :] = kp_ref[...]
    kk_ref[BAND:, :] = kc_ref[...]
    vt_ref[:, 0:BAND] = vp_ref[...].T
    vt_ref[:, BAND:] = vc_ref[...].T
    top = lax.broadcasted_iota(jnp.int32, (LANES, 1), 0) < A_HEAD_DIM
    first = (blk == 0).astype(jnp.int32)
    pair_cols = [slice(j * LANES, (j + 1) * LANES) for j in range(HEAD_PAIRS)]
    for c0 in range(0, nsub, group):
        blocks = [(c, j, slice(c * BAND, (c + 2) * BAND), pair_cols[j])
                  for c in range(c0, c0 + group) for j in range(HEAD_PAIRS)]
        sts = [_dot_nt(kk_ref[keys_rows, cols], qq_ref[c, :, cols]) + bias_ref[first if c == 0 else 0, j]
               for c, j, keys_rows, cols in blocks]
        ms = [jnp.max(st, axis=0, keepdims=True) for st in sts]
        ps = [jnp.exp(st - m) for st, m in zip(sts, ms)]
        ls = [jnp.sum(p, axis=0, keepdims=True) for p in ps]
        ot2s = [_dot(vt_ref[cols, keys_rows], p.astype(BF16)) for (_, _, keys_rows, cols), p in zip(blocks, ps)]
        lses = {}
        for (c, j, _, cols), ot2, m, l in zip(blocks, ot2s, ms, ls):
            rl = 1.0 / l
            ot = jnp.where(top, ot2[:, 0:BAND] * rl[:, 0:BAND], ot2[:, BAND:] * rl[:, BAND:])
            o_ref[c * BAND:(c + 1) * BAND, cols] = ot.T.astype(BF16)
            lse = m + jnp.log(l)
            lses.setdefault(c, []).extend([lse[:, 0:BAND], lse[:, BAND:]])
        for c, rows in lses.items():
            lse_ref[:, c * BAND:(c + 1) * BAND] = jnp.concatenate(rows, axis=0)


def _dilated_bias(dilation):
    qi = np.arange(BAND)[None, :]
    kj = np.arange(2 * BAND)[:, None]
    rel = qi + BAND - kj
    valid = (rel >= 0) & (rel <= BAND)
    slopes = 2.0 ** (-8.0 * np.arange(1, A_HEADS + 1, dtype=np.float64) / A_HEADS)
    bias = -slopes[:, None, None] * (rel * dilation).astype(np.float64)[None]
    gen = np.where(valid[None], bias, NEG)
    fst = np.where((valid & (kj >= BAND))[None], bias, NEG)
    pair = lambda t: np.concatenate([t[0::2], t[1::2]], axis=-1)
    return jnp.asarray(np.stack([pair(gen), pair(fst)]), F32)


def _dilated(aqkv, tq):
    bsz, dilation, n, _ = aqkv.shape
    nsub = tq // BAND
    cur = lambda which: pl.BlockSpec((None, None, tq, A_WIDTH), lambda b, r, i: (b, r, i, which))
    prev = lambda which: pl.BlockSpec((None, None, BAND, A_WIDTH),
                                      lambda b, r, i: (b, r, jnp.maximum(i * nsub - 1, 0), which))
    bias = _dilated_bias(dilation)
    return pl.pallas_call(
        functools.partial(_dilated_kernel, nsub=nsub, group=min(nsub, DILATED_GROUP)),
        grid=(bsz, dilation, n // tq),
        in_specs=[cur(0), cur(1), prev(1), cur(2), prev(2),
                  pl.BlockSpec(bias.shape, lambda b, r, i: (0, 0, 0, 0))],
        out_specs=[pl.BlockSpec((None, None, tq, A_WIDTH), lambda b, r, i: (b, r, i, 0)),
                   pl.BlockSpec((None, None, A_HEADS, tq), lambda b, r, i: (b, r, 0, i))],
        out_shape=[jax.ShapeDtypeStruct((bsz, dilation, n, A_WIDTH), BF16),
                   jax.ShapeDtypeStruct((bsz, dilation, A_HEADS, n), F32)],
        scratch_shapes=[pltpu.VMEM((tq + BAND, A_WIDTH), BF16), pltpu.VMEM((A_WIDTH, tq + BAND), BF16),
                        pltpu.VMEM((nsub, 2 * BAND, A_WIDTH), BF16)],
        compiler_params=pltpu.CompilerParams(dimension_semantics=("arbitrary",) * 3,
                                             vmem_limit_bytes=VMEM_LIMIT),
        name=f"dilated_d{dilation}",
    )(aqkv, aqkv, aqkv, aqkv, aqkv, bias)


def _mla_kernel(qi_ref, kj_ref, q_ref, k_ref, vt_ref, g_ref, o_ref, m_ref, l_ref, acc_ref, *, npair):
    t = pl.program_id(2)
    qi, kj = qi_ref[t], kj_ref[t]

    @pl.when(kj == 0)
    def _():
        m_ref[...] = jnp.full(m_ref.shape, NEG, F32)
        l_ref[...] = jnp.zeros(l_ref.shape, F32)
        acc_ref[...] = jnp.zeros(acc_ref.shape, F32)

    def step(masked):
        heads = range(2 * npair)
        sts = [_dot_nt(k_ref[:, h * LANES:(h + 1) * LANES], q_ref[:, h * LANES:(h + 1) * LANES]) for h in heads]
        if masked:
            kpos = lax.broadcasted_iota(jnp.int32, sts[0].shape, 0)
            qpos = lax.broadcasted_iota(jnp.int32, sts[0].shape, 1)
            sts = [jnp.where(kpos <= qpos, st, NEG) for st in sts]
        m_olds = [m_ref[h] for h in heads]
        m_news = [jnp.maximum(m_old, jnp.max(st, axis=0, keepdims=True)) for m_old, st in zip(m_olds, sts)]
        pts = [jnp.exp(st - m_new) for st, m_new in zip(sts, m_news)]
        alphas = [jnp.exp(m_old - m_new) for m_old, m_new in zip(m_olds, m_news)]
        pvs = [_dot(vt_ref[h * B_V_DIM:(h + 1) * B_V_DIM, :], pt.astype(BF16)) for h, pt in zip(heads, pts)]
        for h in heads:
            rows = slice(h * B_V_DIM, (h + 1) * B_V_DIM)
            l_ref[h] = alphas[h] * l_ref[h] + jnp.sum(pts[h], axis=0, keepdims=True)
            m_ref[h] = m_news[h]
            acc_ref[rows, :] = alphas[h] * acc_ref[rows, :] + pvs[h]

    @pl.when(kj < qi)
    def _():
        step(False)

    @pl.when(kj == qi)
    def _():
        step(True)
        for h in range(2 * npair):
            rows = slice(h * B_V_DIM, (h + 1) * B_V_DIM)
            acc_ref[rows, :] = acc_ref[rows, :] * (1.0 / l_ref[h])
        o_ref[...] = (acc_ref[...].T * g_ref[...].astype(F32)).astype(BF16)


def _mla(qf, kf, vt, gates, tq, npair):
    bsz, seq, _ = qf.shape
    nq = seq // tq
    ngrp = HEAD_PAIRS // npair
    qi = np.concatenate([np.full(i + 1, i) for i in range(nq)]).astype(np.int32)
    kj = np.concatenate([np.arange(i + 1) for i in range(nq)]).astype(np.int32)
    wide, narrow = 2 * npair * LANES, npair * LANES
    grid_spec = pltpu.PrefetchScalarGridSpec(
        num_scalar_prefetch=2,
        grid=(bsz, ngrp, len(qi)),
        in_specs=[pl.BlockSpec((None, tq, wide), lambda b, p, t, qi, kj: (b, qi[t], p)),
                  pl.BlockSpec((None, tq, wide), lambda b, p, t, qi, kj: (b, kj[t], p)),
                  pl.BlockSpec((None, narrow, tq), lambda b, p, t, qi, kj: (b, p, kj[t])),
                  pl.BlockSpec((None, tq, narrow), lambda b, p, t, qi, kj: (b, qi[t], ngrp + p))],
        out_specs=pl.BlockSpec((None, tq, narrow), lambda b, p, t, qi, kj: (b, qi[t], p)),
        scratch_shapes=[pltpu.VMEM((2 * npair, 1, tq), F32), pltpu.VMEM((2 * npair, 1, tq), F32),
                        pltpu.VMEM((narrow, tq), F32)],
    )
    return pl.pallas_call(
        functools.partial(_mla_kernel, npair=npair),
        grid_spec=grid_spec,
        out_shape=jax.ShapeDtypeStruct((bsz, seq, B_WIDTH), BF16),
        compiler_params=pltpu.CompilerParams(dimension_semantics=("arbitrary",) * 3,
                                             vmem_limit_bytes=VMEM_LIMIT),
        name="mla",
    )(jnp.asarray(qi), jnp.asarray(kj), qf, kf, vt, gates)


def _out_kernel(o1_ref, o4_ref, o16_ref, p4_ref, p16_ref, l1_ref, l4_ref, l16_ref, gate_ref, b_ref, x_ref,
                wo_ref, expand_ref, lng_ref, lnb_ref, out_ref):
    tm = out_ref.shape[0]
    unperm = lambda p_ref, o_ref: _dot(p_ref[...], jnp.concatenate([o_ref[r] for r in range(o_ref.shape[0])], axis=0))
    l1, l4, l16 = l1_ref[...], l4_ref[...], l16_ref[...]
    m = jnp.maximum(jnp.maximum(l1, l4), l16)
    e1, e4, e16 = jnp.exp(l1 - m), jnp.exp(l4 - m), jnp.exp(l16 - m)
    rden = 1.0 / (e1 + e4 + e16)

    def expand(wt):
        wt = jnp.concatenate([wt, jnp.zeros_like(wt)], axis=0)
        hi = wt.astype(BF16)
        lo = (wt - hi.astype(F32)).astype(BF16)
        e = expand_ref[...]
        return (lax.dot_general(hi, e, TN_DIMS, preferred_element_type=F32)
                + lax.dot_general(lo, e, TN_DIMS, preferred_element_type=F32))

    a = (expand(e1 * rden) * o1_ref[...].astype(F32) + expand(e4 * rden) * unperm(p4_ref, o4_ref)
         + expand(e16 * rden) * unperm(p16_ref, o16_ref))
    a = a * gate_ref[...].astype(F32)
    y = _dot(a.astype(BF16), wo_ref[0:A_WIDTH, :]) + _dot(b_ref[...], wo_ref[A_WIDTH:, :])
    z = DEEPNORM_ALPHA * x_ref[...] + y
    mu = jnp.mean(z, axis=-1, keepdims=True)
    zc = z - mu
    var = jnp.mean(zc * zc, axis=-1, keepdims=True)
    out_ref[...] = zc * lax.rsqrt(var + LN_EPS) * lng_ref[...] + lnb_ref[...]


def _unpermute_matrix(tm, d):
    p = np.zeros((tm, tm), np.float32)
    t = np.arange(tm)
    p[t, (t % d) * (tm // d) + t // d] = 1.0
    return jnp.asarray(p, BF16)


def _finish(os_, lses, gates, b_out, x, wo, expand, ln_g, ln_b, tm):
    bsz, seq, _ = x.shape
    row = lambda w: pl.BlockSpec((None, tm, w), lambda b, i: (b, i, 0))
    res = lambda d: pl.BlockSpec((None, d, tm // d, A_WIDTH), lambda b, i: (b, 0, i, 0))
    lse = pl.BlockSpec((None, A_HEADS, tm), lambda b, i: (b, 0, i))
    full = lambda a: pl.BlockSpec(a.shape, lambda b, i: (0,) * a.ndim)
    perms = [_unpermute_matrix(tm, d) for d in DILATIONS[1:]]
    return pl.pallas_call(
        _out_kernel,
        grid=(bsz, seq // tm),
        in_specs=[row(A_WIDTH), res(DILATIONS[1]), res(DILATIONS[2]), full(perms[0]), full(perms[1]),
                  lse, lse, lse, row(A_WIDTH), row(B_WIDTH), row(D_MODEL),
                  full(wo), full(expand), full(ln_g), full(ln_b)],
        out_specs=row(D_MODEL),
        out_shape=jax.ShapeDtypeStruct((bsz, seq, D_MODEL), F32),
        compiler_params=pltpu.CompilerParams(dimension_semantics=("arbitrary", "arbitrary"),
                                             vmem_limit_bytes=VMEM_LIMIT),
        name="merge_out_ln",
    )(*os_, *perms, *lses, gates, b_out, x, wo, expand, ln_g, ln_b)


def _rot_cols(w):
    half = w.shape[-1] // 2
    return jnp.concatenate([-w[..., half:], w[..., :half]], axis=-1)


def _layout_weights(w_in, q_norm_g, w_uq, kv_norm_g, w_ukv, w_o):
    aw = A_WIDTH
    a_q, a_k, a_v, a_gate = (w_in[:, i * aw:(i + 1) * aw] for i in range(4))
    o = 4 * aw
    c_q, o = w_in[:, o:o + Q_LORA_RANK], o + Q_LORA_RANK
    c_kv, o = w_in[:, o:o + KV_LORA_RANK], o + KV_LORA_RANK
    k_rope, o = w_in[:, o:o + B_ROPE_DIM], o + B_ROPE_DIM
    b_gate = w_in[:, o:o + B_WIDTH]
    pad = jnp.zeros((D_MODEL, LANES - 2 * B_ROPE_DIM), w_in.dtype)
    win = jnp.concatenate([a_q * (A_HEAD_DIM ** -0.5), a_k, a_v, a_gate, b_gate, c_q, c_kv,
                           k_rope, _rot_cols(k_rope), pad], axis=1).astype(BF16)
    uq = w_uq.reshape(Q_LORA_RANK, B_HEADS, B_NOPE_DIM + B_ROPE_DIM)
    nope, rope = uq[..., :B_NOPE_DIM], uq[..., B_NOPE_DIM:]
    z = lambda w: jnp.zeros((Q_LORA_RANK, B_HEADS, w), w_uq.dtype)
    tail = LANES - B_NOPE_DIM - B_ROPE_DIM
    w1 = jnp.concatenate([nope, rope, z(tail)], axis=-1).reshape(Q_LORA_RANK, B_HEADS * LANES)
    w2 = jnp.concatenate([z(B_NOPE_DIM), _rot_cols(rope), z(tail)], axis=-1).reshape(Q_LORA_RANK, B_HEADS * LANES)
    wq = jnp.concatenate([w1, w2], axis=1).astype(BF16)
    ukv = w_ukv.reshape(KV_LORA_RANK, B_HEADS, B_NOPE_DIM + B_V_DIM)
    wk = jnp.concatenate([ukv[..., :B_NOPE_DIM], jnp.zeros((KV_LORA_RANK, B_HEADS, LANES - B_NOPE_DIM), w_ukv.dtype)],
                         axis=-1).reshape(KV_LORA_RANK, B_HEADS * LANES)
    wvt = ukv[..., B_NOPE_DIM:].reshape(KV_LORA_RANK, B_WIDTH).T
    place = np.zeros((LANES, B_HEADS * LANES), np.float32)
    for h in range(B_HEADS):
        place[np.arange(B_ROPE_DIM), h * LANES + B_NOPE_DIM + np.arange(B_ROPE_DIM)] = 1.0
    expand = np.zeros((2 * SUBLANES, A_WIDTH), np.float32)
    for h in range(A_HEADS):
        expand[h, h * A_HEAD_DIM:(h + 1) * A_HEAD_DIM] = 1.0
    return {"win": win, "wq": wq, "wk": wk.astype(BF16), "wvt": wvt.astype(BF16), "place": jnp.asarray(place, BF16),
            "gq": q_norm_g.reshape(1, -1).astype(F32), "gkv": kv_norm_g.reshape(1, -1).astype(F32),
            "wo": w_o.astype(BF16), "expand": jnp.asarray(expand, BF16)}


def _rope_tables(seq):
    inv_freq = ROPE_THETA ** (-jnp.arange(0, B_ROPE_DIM, 2, dtype=F32) / B_ROPE_DIM)
    ang = jnp.arange(seq, dtype=F32)[:, None] * inv_freq[None, :]
    ang = jnp.concatenate([ang, ang], axis=-1)
    cos, sin = jnp.cos(ang), jnp.sin(ang)
    scale = (B_NOPE_DIM + B_ROPE_DIM) ** -0.5
    one = jnp.ones((seq, B_NOPE_DIM), F32)
    zn = jnp.zeros((seq, B_NOPE_DIM), F32)
    zt = jnp.zeros((seq, LANES - B_NOPE_DIM - B_ROPE_DIM), F32)
    cosq = jnp.tile(jnp.concatenate([one, cos, zt], axis=1) * scale, (1, B_HEADS))
    sinq = jnp.tile(jnp.concatenate([zn, sin, zt], axis=1) * scale, (1, B_HEADS))
    csk = jnp.concatenate([cos, sin, jnp.zeros((seq, LANES - 2 * B_ROPE_DIM), F32)], axis=1)
    return {"csk": csk, "cosq": cosq, "sinq": sinq}


def kernel(x, w_in, q_norm_g, w_uq, kv_norm_g, w_ukv, w_o, ln_g, ln_b):
    bsz, seq, _ = x.shape
    wts = _layout_weights(w_in, q_norm_g, w_uq, kv_norm_g, w_ukv, w_o)
    tabs = _rope_tables(seq)
    aqkv1, aqkv4, aqkv16, gates, qf, kf, vt = _project(x, wts, tabs, tm=512)
    os_, lses = [], []
    for aqkv in (aqkv1[:, None], aqkv4, aqkv16):
        d, n = aqkv.shape[1], aqkv.shape[2]
        o, lse_t = _dilated(aqkv, tq=min(512, n))
        os_.append(o)
        lses.append(jnp.transpose(lse_t, (0, 2, 3, 1)).reshape(bsz, A_HEADS, seq))
    os_[0] = os_[0][:, 0]
    b_out = _mla(qf, kf, vt, gates, tq=512, npair=2)
    return _finish(os_, lses, gates, b_out, x, wts["wo"], wts["expand"],
                   ln_g.reshape(1, -1).astype(F32), ln_b.reshape(1, -1).astype(F32), tm=512)
```

```python
import functools

import jax
import jax.numpy as jnp
import numpy as np
from jax import lax
from jax.experimental import pallas as pl
from jax.experimental.pallas import tpu as pltpu

D_MODEL = 1024
A_HEADS = 8
A_HEAD_DIM = 64
A_WIDTH = A_HEADS * A_HEAD_DIM
DILATED_PATTERNS = ((128, 1), (512, 4), (2048, 16))
DILATIONS = tuple(d for _, d in DILATED_PATTERNS)
BAND = 128
B_HEADS = 8
B_NOPE_DIM = 64
B_ROPE_DIM = 32
B_V_DIM = 64
B_WIDTH = B_HEADS * B_V_DIM
Q_LORA_RANK = 256
KV_LORA_RANK = 128
ROPE_THETA = 10000.0
LN_EPS = 1e-5
RMS_EPS = 1e-6
DEPTH = 1
DEEPNORM_ALPHA = (2 * DEPTH) ** 0.25

LANES = 128
SUBLANES = 8
HEAD_PAIRS = A_HEADS // 2
NEG = -1e30
VMEM_LIMIT = 56 * 1024 * 1024

BF16 = jnp.bfloat16
F32 = jnp.float32

C_AQKV = 0
C_GATE = 1536
C_CQ = 2560
C_CKV = 2816
C_KR = 2944
IN_COLS = 3072
AQKV_SLABS = 3 * A_WIDTH // LANES

TN_DIMS = (((0,), (0,)), ((), ()))
LOG2E = float(np.log2(np.e))
ONES_ROWS = 16
ACC_ROWS = B_V_DIM + ONES_ROWS
MLA_LOOKAHEAD = 2
DILATED_GROUP = 4


def _dot(a, b):
    return jnp.dot(a, b, preferred_element_type=F32)


def _dot_nt(a, b):
    return lax.dot_general(a, b, (((1,), (1,)), ((), ())), preferred_element_type=F32)


def _rms(t, g):
    return t * lax.rsqrt(jnp.mean(t * t, axis=-1, keepdims=True) + RMS_EPS) * g


def _silu(t):
    return t / (1.0 + jnp.exp(-t))


def _proj_kernel(x_ref, win_ref, wq_ref, wk_ref, wvt_ref, place_ref, gq_ref, gkv_ref, csk_ref, cosq_ref, sinq_ref,
                 aqkv1_ref, aqkv4_ref, aqkv16_ref, gate_ref, qf_ref, kf_ref, vt_ref, slab_ref):
    xb = x_ref[...].astype(BF16)
    tm = xb.shape[0]
    aqkv = _dot(xb, win_ref[:, C_AQKV:C_GATE])
    aqkv1_ref[...] = aqkv.astype(BF16)
    for sl in range(AQKV_SLABS):
        slab_ref[sl] = aqkv[:, sl * LANES:(sl + 1) * LANES]
    for d, ref in ((DILATIONS[1], aqkv4_ref), (DILATIONS[2], aqkv16_ref)):
        for r in range(d):
            for sl in range(AQKV_SLABS):
                ref[r, :, sl * LANES:(sl + 1) * LANES] = slab_ref[sl, pl.ds(r, tm // d, stride=d), :].astype(BF16)
    gate_ref[...] = _silu(_dot(xb, win_ref[:, C_GATE:C_CQ])).astype(BF16)
    lat = _dot(xb, win_ref[:, C_CQ:IN_COLS])
    cq = lat[:, 0:Q_LORA_RANK]
    ckv = lat[:, Q_LORA_RANK:Q_LORA_RANK + KV_LORA_RANK]
    rope = lat[:, Q_LORA_RANK + KV_LORA_RANK:]
    q2 = _dot(_rms(cq, gq_ref[...]).astype(BF16), wq_ref[...])
    qf_ref[...] = (q2[:, :B_HEADS * LANES] * cosq_ref[...] + q2[:, B_HEADS * LANES:] * sinq_ref[...]).astype(BF16)
    r2 = rope * csk_ref[...]
    kr = r2 + pltpu.roll(r2, LANES - B_ROPE_DIM, axis=1)
    kvn = _rms(ckv, gkv_ref[...]).astype(BF16)
    kf_ref[...] = (_dot(kvn, wk_ref[...]) + _dot(kr.astype(BF16), place_ref[...])).astype(BF16)
    vt_ref[...] = _dot_nt(wvt_ref[...], kvn).astype(BF16)


def _project(x, wts, tabs, tm):
    bsz, seq, _ = x.shape
    row = lambda w: pl.BlockSpec((None, tm, w), lambda i, b: (b, i, 0))
    res = lambda d: pl.BlockSpec((None, d, tm // d, 3 * A_WIDTH), lambda i, b: (b, 0, i, 0))
    full = lambda a: pl.BlockSpec(a.shape, lambda i, b: (0,) * a.ndim)
    tab = lambda w: pl.BlockSpec((tm, w), lambda i, b: (i, 0))
    out = lambda w: jax.ShapeDtypeStruct((bsz, seq, w), BF16)
    res_out = lambda d: jax.ShapeDtypeStruct((bsz, d, seq // d, 3 * A_WIDTH), BF16)
    return pl.pallas_call(
        _proj_kernel,
        grid=(seq // tm, bsz),
        in_specs=[row(D_MODEL), full(wts["win"]), full(wts["wq"]), full(wts["wk"]), full(wts["wvt"]),
                  full(wts["place"]), full(wts["gq"]), full(wts["gkv"]),
                  tab(LANES), tab(B_HEADS * LANES), tab(B_HEADS * LANES)],
        out_specs=[row(3 * A_WIDTH), res(DILATIONS[1]), res(DILATIONS[2]), row(A_WIDTH + B_WIDTH),
                   row(B_HEADS * LANES), row(B_HEADS * LANES),
                   pl.BlockSpec((None, B_WIDTH, tm), lambda i, b: (b, 0, i))],
        out_shape=[out(3 * A_WIDTH), res_out(DILATIONS[1]), res_out(DILATIONS[2]), out(A_WIDTH + B_WIDTH),
                   out(B_HEADS * LANES), out(B_HEADS * LANES),
                   jax.ShapeDtypeStruct((bsz, B_WIDTH, seq), BF16)],
        scratch_shapes=[pltpu.VMEM((AQKV_SLABS, tm, LANES), F32)],
        compiler_params=pltpu.CompilerParams(dimension_semantics=("arbitrary", "arbitrary"),
                                             vmem_limit_bytes=VMEM_LIMIT),
        name="proj",
    )(x, wts["win"], wts["wq"], wts["wk"], wts["wvt"], wts["place"], wts["gq"], wts["gkv"],
      tabs["csk"], tabs["cosq"], tabs["sinq"])


def _dilated_kernel(q_ref, kc_ref, kp_ref, vc_ref, vp_ref, bias_ref, o_ref, lse_ref,
                    kk_ref, vt_ref, qq_ref, *, nsub, group):
    blk = pl.program_id(2)
    lane = lax.broadcasted_iota(jnp.int32, (1, A_WIDTH), 1) % LANES
    lo = lane < A_HEAD_DIM
    for c in range(nsub):
        q = q_ref[c * BAND:(c + 1) * BAND, :]
        qq_ref[c, 0:BAND, :] = jnp.where(lo, q, jnp.zeros_like(q))
        qq_ref[c, BAND:, :] = jnp.where(lo, jnp.zeros_like(q), q)
    kk_ref[0:BAND, :] = kp_ref[...]
    kk_ref[BAND:, :] = kc_ref[...]
    vt_ref[:, 0:BAND] = vp_ref[...].T
    vt_ref[:, BAND:] = vc_ref[...].T
    top = lax.broadcasted_iota(jnp.int32, (LANES, 1), 0) < A_HEAD_DIM
    first = (blk == 0).astype(jnp.int32)
    pair_cols = [slice(j * LANES, (j + 1) * LANES) for j in range(HEAD_PAIRS)]
    for c0 in range(0, nsub, group):
        blocks = [(c, j, slice(c * BAND, (c + 2) * BAND), pair_cols[j])
                  for c in range(c0, c0 + group) for j in range(HEAD_PAIRS)]
        sts = [_dot_nt(kk_ref[keys_rows, cols], qq_ref[c, :, cols]) + bias_ref[first if c == 0 else 0, j]
               for c, j, keys_rows, cols in blocks]
        ms = [jnp.max(st, axis=0, keepdims=True) for st in sts]
        ps = [jnp.exp2(st - m).astype(BF16) for st, m in zip(sts, ms)]
        ones = jnp.ones((ONES_ROWS, 2 * BAND), BF16)
        ot2s = [_dot(jnp.concatenate([vt_ref[cols, keys_rows], ones], axis=0), p)
                for (_, _, keys_rows, cols), p in zip(blocks, ps)]
        lses = {}
        for (c, j, _, cols), ot2, m in zip(blocks, ot2s, ms):
            l = ot2[LANES:LANES + 1, :]
            rl = 1.0 / l
            ot = jnp.where(top, ot2[0:LANES, 0:BAND] * rl[:, 0:BAND], ot2[0:LANES, BAND:] * rl[:, BAND:])
            o_ref[c * BAND:(c + 1) * BAND, cols] = ot.T.astype(BF16)
            lse = m + jnp.log2(l)
            lses.setdefault(c, []).extend([lse[:, 0:BAND], lse[:, BAND:]])
        for c, rows in lses.items():
            lse_ref[:, c * BAND:(c + 1) * BAND] = jnp.concatenate(rows, axis=0)


def _dilated_bias(dilation):
    qi = np.arange(BAND)[None, :]
    kj = np.arange(2 * BAND)[:, None]
    rel = qi + BAND - kj
    valid = (rel >= 0) & (rel <= BAND)
    slopes = 2.0 ** (-8.0 * np.arange(1, A_HEADS + 1, dtype=np.float64) / A_HEADS)
    bias = -slopes[:, None, None] * (rel * dilation).astype(np.float64)[None] * LOG2E
    gen = np.where(valid[None], bias, NEG)
    fst = np.where((valid & (kj >= BAND))[None], bias, NEG)
    pair = lambda t: np.concatenate([t[0::2], t[1::2]], axis=-1)
    return jnp.asarray(np.stack([pair(gen), pair(fst)]), F32)


def _dilated(aqkv, tq):
    bsz, dilation, n, _ = aqkv.shape
    nsub = tq // BAND
    cur = lambda which: pl.BlockSpec((None, None, tq, A_WIDTH), lambda b, r, i: (b, r, i, which))
    prev = lambda which: pl.BlockSpec((None, None, BAND, A_WIDTH),
                                      lambda b, r, i: (b, r, jnp.maximum(i * nsub - 1, 0), which))
    bias = _dilated_bias(dilation)
    return pl.pallas_call(
        functools.partial(_dilated_kernel, nsub=nsub, group=min(nsub, DILATED_GROUP)),
        grid=(bsz, dilation, n // tq),
        in_specs=[cur(0), cur(1), prev(1), cur(2), prev(2),
                  pl.BlockSpec(bias.shape, lambda b, r, i: (0, 0, 0, 0))],
        out_specs=[pl.BlockSpec((None, None, tq, A_WIDTH), lambda b, r, i: (b, r, i, 0)),
                   pl.BlockSpec((None, None, A_HEADS, tq), lambda b, r, i: (b, r, 0, i))],
        out_shape=[jax.ShapeDtypeStruct((bsz, dilation, n, A_WIDTH), BF16),
                   jax.ShapeDtypeStruct((bsz, dilation, A_HEADS, n), F32)],
        scratch_shapes=[pltpu.VMEM((tq + BAND, A_WIDTH), BF16), pltpu.VMEM((A_WIDTH, tq + BAND), BF16),
                        pltpu.VMEM((nsub, 2 * BAND, A_WIDTH), BF16)],
        compiler_params=pltpu.CompilerParams(dimension_semantics=("arbitrary",) * 3,
                                             vmem_limit_bytes=VMEM_LIMIT),
        name=f"dilated_d{dilation}",
    )(aqkv, aqkv, aqkv, aqkv, aqkv, bias)


def _mla_kernel(qi_ref, kj_ref, q_ref, k_ref, vt_ref, g_ref, o_ref, m_ref, acc_ref, *, npair):
    t = pl.program_id(2)
    qi, kj = qi_ref[t], kj_ref[t]
    heads = range(2 * npair)
    acc_rows = lambda h: slice(h * ACC_ROWS, (h + 1) * ACC_ROWS)

    @pl.when(kj == 0)
    def _():
        m_ref[...] = jnp.full(m_ref.shape, NEG, F32)
        acc_ref[...] = jnp.zeros(acc_ref.shape, F32)

    def scores(h, masked):
        st = _dot_nt(k_ref[:, h * LANES:(h + 1) * LANES], q_ref[:, h * LANES:(h + 1) * LANES])
        if masked:
            kpos = lax.broadcasted_iota(jnp.int32, st.shape, 0)
            qpos = lax.broadcasted_iota(jnp.int32, st.shape, 1)
            st = jnp.where(kpos <= qpos, st, NEG)
        return st

    def accumulate(h, st):
        m_old = m_ref[h]
        m_new = jnp.maximum(m_old, jnp.max(st, axis=0, keepdims=True))
        pt = jnp.exp2(st - m_new).astype(BF16)
        alpha = jnp.exp2(m_old - m_new)
        ones = jnp.ones((ONES_ROWS, vt_ref.shape[1]), BF16)
        pv = _dot(jnp.concatenate([vt_ref[h * B_V_DIM:(h + 1) * B_V_DIM, :], ones], axis=0), pt)
        m_ref[h] = m_new
        acc_ref[acc_rows(h), :] = alpha * acc_ref[acc_rows(h), :] + pv

    def step(masked):
        pending = {}
        for i in range(len(heads) + MLA_LOOKAHEAD):
            if i < len(heads):
                pending[i] = scores(heads[i], masked)
            if i >= MLA_LOOKAHEAD:
                accumulate(heads[i - MLA_LOOKAHEAD], pending.pop(i - MLA_LOOKAHEAD))

    @pl.when(kj < qi)
    def _():
        step(False)

    @pl.when(kj == qi)
    def _():
        step(True)
        ot = jnp.concatenate([acc_ref[h * ACC_ROWS:h * ACC_ROWS + B_V_DIM, :]
                              * (1.0 / acc_ref[h * ACC_ROWS + B_V_DIM:h * ACC_ROWS + B_V_DIM + 1, :])
                              for h in heads], axis=0)
        o_ref[...] = (ot.T * g_ref[...].astype(F32)).astype(BF16)


def _mla(qf, kf, vt, gates, tq, npair):
    bsz, seq, _ = qf.shape
    nq = seq // tq
    ngrp = HEAD_PAIRS // npair
    qi = np.concatenate([np.full(i + 1, i) for i in range(nq)]).astype(np.int32)
    kj = np.concatenate([np.arange(i + 1) for i in range(nq)]).astype(np.int32)
    wide, narrow = 2 * npair * LANES, npair * LANES
    grid_spec = pltpu.PrefetchScalarGridSpec(
        num_scalar_prefetch=2,
        grid=(bsz, ngrp, len(qi)),
        in_specs=[pl.BlockSpec((None, tq, wide), lambda b, p, t, qi, kj: (b, qi[t], p)),
                  pl.BlockSpec((None, tq, wide), lambda b, p, t, qi, kj: (b, kj[t], p)),
                  pl.BlockSpec((None, narrow, tq), lambda b, p, t, qi, kj: (b, p, kj[t])),
                  pl.BlockSpec((None, tq, narrow), lambda b, p, t, qi, kj: (b, qi[t], ngrp + p))],
        out_specs=pl.BlockSpec((None, tq, narrow), lambda b, p, t, qi, kj: (b, qi[t], p)),
        scratch_shapes=[pltpu.VMEM((2 * npair, 1, tq), F32), pltpu.VMEM((2 * npair * ACC_ROWS, tq), F32)],
    )
    return pl.pallas_call(
        functools.partial(_mla_kernel, npair=npair),
        grid_spec=grid_spec,
        out_shape=jax.ShapeDtypeStruct((bsz, seq, B_WIDTH), BF16),
        compiler_params=pltpu.CompilerParams(dimension_semantics=("arbitrary",) * 3,
                                             vmem_limit_bytes=VMEM_LIMIT),
        name="mla",
    )(jnp.asarray(qi), jnp.asarray(kj), qf, kf, vt, gates)


def _out_kernel(o1_ref, o4_ref, o16_ref, p4_ref, p16_ref, l1_ref, l4_ref, l16_ref, gate_ref, b_ref, x_ref,
                wo_ref, expand_ref, lng_ref, lnb_ref, out_ref):
    tm = out_ref.shape[0]
    unperm = lambda p_ref, o_ref: _dot(p_ref[...], jnp.concatenate([o_ref[r] for r in range(o_ref.shape[0])], axis=0))
    l1, l4, l16 = l1_ref[...], l4_ref[...], l16_ref[...]
    m = jnp.maximum(jnp.maximum(l1, l4), l16)
    e1, e4, e16 = jnp.exp2(l1 - m), jnp.exp2(l4 - m), jnp.exp2(l16 - m)
    rden = 1.0 / (e1 + e4 + e16)

    def expand(wt):
        wt = jnp.concatenate([wt, jnp.zeros_like(wt)], axis=0)
        hi = wt.astype(BF16)
        lo = (wt - hi.astype(F32)).astype(BF16)
        e = expand_ref[...]
        return (lax.dot_general(hi, e, TN_DIMS, preferred_element_type=F32)
                + lax.dot_general(lo, e, TN_DIMS, preferred_element_type=F32))

    a = (expand(e1 * rden) * o1_ref[...].astype(F32) + expand(e4 * rden) * unperm(p4_ref, o4_ref)
         + expand(e16 * rden) * unperm(p16_ref, o16_ref))
    a = a * gate_ref[...].astype(F32)
    y = _dot(a.astype(BF16), wo_ref[0:A_WIDTH, :]) + _dot(b_ref[...], wo_ref[A_WIDTH:, :])
    z = DEEPNORM_ALPHA * x_ref[...] + y
    mu = jnp.mean(z, axis=-1, keepdims=True)
    zc = z - mu
    var = jnp.mean(zc * zc, axis=-1, keepdims=True)
    out_ref[...] = zc * lax.rsqrt(var + LN_EPS) * lng_ref[...] + lnb_ref[...]


def _unpermute_matrix(tm, d):
    p = np.zeros((tm, tm), np.float32)
    t = np.arange(tm)
    p[t, (t % d) * (tm // d) + t // d] = 1.0
    return jnp.asarray(p, BF16)


def _finish(os_, lses, gates, b_out, x, wo, expand, ln_g, ln_b, tm):
    bsz, seq, _ = x.shape
    row = lambda w: pl.BlockSpec((None, tm, w), lambda b, i: (b, i, 0))
    res = lambda d: pl.BlockSpec((None, d, tm // d, A_WIDTH), lambda b, i: (b, 0, i, 0))
    lse = pl.BlockSpec((None, A_HEADS, tm), lambda b, i: (b, 0, i))
    full = lambda a: pl.BlockSpec(a.shape, lambda b, i: (0,) * a.ndim)
    perms = [_unpermute_matrix(tm, d) for d in DILATIONS[1:]]
    return pl.pallas_call(
        _out_kernel,
        grid=(bsz, seq // tm),
        in_specs=[row(A_WIDTH), res(DILATIONS[1]), res(DILATIONS[2]), full(perms[0]), full(perms[1]),
                  lse, lse, lse, row(A_WIDTH), row(B_WIDTH), row(D_MODEL),
                  full(wo), full(expand), full(ln_g), full(ln_b)],
        out_specs=row(D_MODEL),
        out_shape=jax.ShapeDtypeStruct((bsz, seq, D_MODEL), F32),
        compiler_params=pltpu.CompilerParams(dimension_semantics=("arbitrary", "arbitrary"),
                                             vmem_limit_bytes=VMEM_LIMIT),
        name="merge_out_ln",
    )(*os_, *perms, *lses, gates, b_out, x, wo, expand, ln_g, ln_b)


def _rot_cols(w):
    half = w.shape[-1] // 2
    return jnp.concatenate([-w[..., half:], w[..., :half]], axis=-1)


def _layout_weights(w_in, q_norm_g, w_uq, kv_norm_g, w_ukv, w_o):
    aw = A_WIDTH
    a_q, a_k, a_v, a_gate = (w_in[:, i * aw:(i + 1) * aw] for i in range(4))
    o = 4 * aw
    c_q, o = w_in[:, o:o + Q_LORA_RANK], o + Q_LORA_RANK
    c_kv, o = w_in[:, o:o + KV_LORA_RANK], o + KV_LORA_RANK
    k_rope, o = w_in[:, o:o + B_ROPE_DIM], o + B_ROPE_DIM
    b_gate = w_in[:, o:o + B_WIDTH]
    pad = jnp.zeros((D_MODEL, LANES - 2 * B_ROPE_DIM), w_in.dtype)
    win = jnp.concatenate([a_q * (A_HEAD_DIM ** -0.5 * LOG2E), a_k, a_v, a_gate, b_gate, c_q, c_kv,
                           k_rope, _rot_cols(k_rope), pad], axis=1).astype(BF16)
    uq = w_uq.reshape(Q_LORA_RANK, B_HEADS, B_NOPE_DIM + B_ROPE_DIM)
    nope, rope = uq[..., :B_NOPE_DIM], uq[..., B_NOPE_DIM:]
    z = lambda w: jnp.zeros((Q_LORA_RANK, B_HEADS, w), w_uq.dtype)
    tail = LANES - B_NOPE_DIM - B_ROPE_DIM
    w1 = jnp.concatenate([nope, rope, z(tail)], axis=-1).reshape(Q_LORA_RANK, B_HEADS * LANES)
    w2 = jnp.concatenate([z(B_NOPE_DIM), _rot_cols(rope), z(tail)], axis=-1).reshape(Q_LORA_RANK, B_HEADS * LANES)
    wq = jnp.concatenate([w1, w2], axis=1).astype(BF16)
    ukv = w_ukv.reshape(KV_LORA_RANK, B_HEADS, B_NOPE_DIM + B_V_DIM)
    wk = jnp.concatenate([ukv[..., :B_NOPE_DIM], jnp.zeros((KV_LORA_RANK, B_HEADS, LANES - B_NOPE_DIM), w_ukv.dtype)],
                         axis=-1).reshape(KV_LORA_RANK, B_HEADS * LANES)
    wvt = ukv[..., B_NOPE_DIM:].reshape(KV_LORA_RANK, B_WIDTH).T
    place = np.zeros((LANES, B_HEADS * LANES), np.float32)
    for h in range(B_HEADS):
        place[np.arange(B_ROPE_DIM), h * LANES + B_NOPE_DIM + np.arange(B_ROPE_DIM)] = 1.0
    expand = np.zeros((2 * SUBLANES, A_WIDTH), np.float32)
    for h in range(A_HEADS):
        expand[h, h * A_HEAD_DIM:(h + 1) * A_HEAD_DIM] = 1.0
    return {"win": win, "wq": wq, "wk": wk.astype(BF16), "wvt": wvt.astype(BF16), "place": jnp.asarray(place, BF16),
            "gq": q_norm_g.reshape(1, -1).astype(F32), "gkv": kv_norm_g.reshape(1, -1).astype(F32),
            "wo": w_o.astype(BF16), "expand": jnp.asarray(expand, BF16)}


def _rope_tables(seq):
    inv_freq = ROPE_THETA ** (-jnp.arange(0, B_ROPE_DIM, 2, dtype=F32) / B_ROPE_DIM)
    ang = jnp.arange(seq, dtype=F32)[:, None] * inv_freq[None, :]
    ang = jnp.concatenate([ang, ang], axis=-1)
    cos, sin = jnp.cos(ang), jnp.sin(ang)
    scale = (B_NOPE_DIM + B_ROPE_DIM) ** -0.5 * LOG2E
    one = jnp.ones((seq, B_NOPE_DIM), F32)
    zn = jnp.zeros((seq, B_NOPE_DIM), F32)
    zt = jnp.zeros((seq, LANES - B_NOPE_DIM - B_ROPE_DIM), F32)
    cosq = jnp.tile(jnp.concatenate([one, cos, zt], axis=1) * scale, (1, B_HEADS))
    sinq = jnp.tile(jnp.concatenate([zn, sin, zt], axis=1) * scale, (1, B_HEADS))
    csk = jnp.concatenate([cos, sin, jnp.zeros((seq, LANES - 2 * B_ROPE_DIM), F32)], axis=1)
    return {"csk": csk, "cosq": cosq, "sinq": sinq}


def kernel(x, w_in, q_norm_g, w_uq, kv_norm_g, w_ukv, w_o, ln_g, ln_b):
    bsz, seq, _ = x.shape
    wts = _layout_weights(w_in, q_norm_g, w_uq, kv_norm_g, w_ukv, w_o)
    tabs = _rope_tables(seq)
    aqkv1, aqkv4, aqkv16, gates, qf, kf, vt = _project(x, wts, tabs, tm=512)
    os_, lses = [], []
    for aqkv in (aqkv1[:, None], aqkv4, aqkv16):
        d, n = aqkv.shape[1], aqkv.shape[2]
        o, lse_t = _dilated(aqkv, tq=min(512, n))
        os_.append(o)
        lses.append(jnp.transpose(lse_t, (0, 2, 3, 1)).reshape(bsz, A_HEADS, seq))
    os_[0] = os_[0][:, 0]
    b_out = _mla(qf, kf, vt, gates, tq=512, npair=4)
    return _finish(os_, lses, gates, b_out, x, wts["wo"], wts["expand"],
                   ln_g.reshape(1, -1).astype(F32), ln_b.reshape(1, -1).astype(F32), tm=512)
```

```python
import functools

import jax
import jax.numpy as jnp
import numpy as np
from jax import lax
from jax.experimental import pallas as pl
from jax.experimental.pallas import tpu as pltpu

D_MODEL = 1024
A_HEADS = 8
A_HEAD_DIM = 64
A_WIDTH = A_HEADS * A_HEAD_DIM
DILATED_PATTERNS = ((128, 1), (512, 4), (2048, 16))
DILATIONS = tuple(d for _, d in DILATED_PATTERNS)
BAND = 128
B_HEADS = 8
B_NOPE_DIM = 64
B_ROPE_DIM = 32
B_V_DIM = 64
B_WIDTH = B_HEADS * B_V_DIM
Q_LORA_RANK = 256
KV_LORA_RANK = 128
ROPE_THETA = 10000.0
LN_EPS = 1e-5
RMS_EPS = 1e-6
DEPTH = 1
DEEPNORM_ALPHA = (2 * DEPTH) ** 0.25

LANES = 128
SUBLANES = 8
HEAD_PAIRS = A_HEADS // 2
NEG = -1e30
VMEM_LIMIT = 56 * 1024 * 1024

BF16 = jnp.bfloat16
F32 = jnp.float32

C_AQKV = 0
C_GATE = 1536
C_CQ = 2560
C_CKV = 2816
C_KR = 2944
IN_COLS = 3072
AQKV_SLABS = 3 * A_WIDTH // LANES

TN_DIMS = (((0,), (0,)), ((), ()))
LOG2E = float(np.log2(np.e))
ONES_ROWS = 16
ACC_ROWS = B_V_DIM + ONES_ROWS
MLA_LOOKAHEAD = 2
OUT_CHUNK = 128
DILATED_GROUP = 4


def _dot(a, b):
    return jnp.dot(a, b, preferred_element_type=F32)


def _dot_nt(a, b):
    return lax.dot_general(a, b, (((1,), (1,)), ((), ())), preferred_element_type=F32)


def _rms(t, g):
    return t * lax.rsqrt(jnp.mean(t * t, axis=-1, keepdims=True) + RMS_EPS) * g


def _silu(t):
    return t / (1.0 + jnp.exp(-t))


def _proj_kernel(x_ref, win_ref, wq_ref, wk_ref, wvt_ref, gq_ref, gkv_ref, csk_ref, cosq_ref, sinq_ref,
                 aqkv1_ref, aqkv4_ref, aqkv16_ref, gate_ref, qf_ref, kf_ref, vt_ref, slab_ref):
    xb = x_ref[...].astype(BF16)
    tm = xb.shape[0]
    lane_tile = lax.broadcasted_iota(jnp.int32, (1, LANES), 1)
    aqkv = _dot(xb, win_ref[:, C_AQKV:C_GATE])
    aqkv1_ref[...] = aqkv.astype(BF16)
    for sl in range(AQKV_SLABS):
        slab_ref[sl] = aqkv[:, sl * LANES:(sl + 1) * LANES]
    for d, ref in ((DILATIONS[1], aqkv4_ref), (DILATIONS[2], aqkv16_ref)):
        for r in range(d):
            for sl in range(AQKV_SLABS):
                ref[r, :, sl * LANES:(sl + 1) * LANES] = slab_ref[sl, pl.ds(r, tm // d, stride=d), :].astype(BF16)
    gate_ref[...] = _silu(_dot(xb, win_ref[:, C_GATE:C_CQ])).astype(BF16)
    lat = _dot(xb, win_ref[:, C_CQ:IN_COLS])
    cq = lat[:, 0:Q_LORA_RANK]
    ckv = lat[:, Q_LORA_RANK:Q_LORA_RANK + KV_LORA_RANK]
    rope = lat[:, Q_LORA_RANK + KV_LORA_RANK:]
    q2 = _dot(_rms(cq, gq_ref[...]).astype(BF16), wq_ref[...])
    qf_ref[...] = (q2[:, :B_HEADS * LANES] * cosq_ref[...] + q2[:, B_HEADS * LANES:] * sinq_ref[...]).astype(BF16)
    r2 = rope * csk_ref[...]
    kr = pltpu.roll(r2, B_NOPE_DIM, axis=1) + pltpu.roll(r2, B_NOPE_DIM - B_ROPE_DIM, axis=1)
    kr = jnp.where((lane_tile >= B_NOPE_DIM) & (lane_tile < B_NOPE_DIM + B_ROPE_DIM), kr, 0.0)
    kvn = _rms(ckv, gkv_ref[...]).astype(BF16)
    kn = _dot(kvn, wk_ref[...])
    for h in range(B_HEADS):
        kf_ref[:, h * LANES:(h + 1) * LANES] = (kn[:, h * LANES:(h + 1) * LANES] + kr).astype(BF16)
    vt_ref[...] = _dot_nt(wvt_ref[...], kvn).astype(BF16)


def _project(x, wts, tabs, tm):
    bsz, seq, _ = x.shape
    row = lambda w: pl.BlockSpec((None, tm, w), lambda i, b: (b, i, 0))
    res = lambda d: pl.BlockSpec((None, d, tm // d, 3 * A_WIDTH), lambda i, b: (b, 0, i, 0))
    full = lambda a: pl.BlockSpec(a.shape, lambda i, b: (0,) * a.ndim)
    tab = lambda w: pl.BlockSpec((tm, w), lambda i, b: (i, 0))
    out = lambda w: jax.ShapeDtypeStruct((bsz, seq, w), BF16)
    res_out = lambda d: jax.ShapeDtypeStruct((bsz, d, seq // d, 3 * A_WIDTH), BF16)
    return pl.pallas_call(
        _proj_kernel,
        grid=(seq // tm, bsz),
        in_specs=[row(D_MODEL), full(wts["win"]), full(wts["wq"]), full(wts["wk"]), full(wts["wvt"]),
                  full(wts["gq"]), full(wts["gkv"]),
                  tab(LANES), tab(B_HEADS * LANES), tab(B_HEADS * LANES)],
        out_specs=[row(3 * A_WIDTH), res(DILATIONS[1]), res(DILATIONS[2]), row(A_WIDTH + B_WIDTH),
                   row(B_HEADS * LANES), row(B_HEADS * LANES),
                   pl.BlockSpec((None, B_WIDTH, tm), lambda i, b: (b, 0, i))],
        out_shape=[out(3 * A_WIDTH), res_out(DILATIONS[1]), res_out(DILATIONS[2]), out(A_WIDTH + B_WIDTH),
                   out(B_HEADS * LANES), out(B_HEADS * LANES),
                   jax.ShapeDtypeStruct((bsz, B_WIDTH, seq), BF16)],
        scratch_shapes=[pltpu.VMEM((AQKV_SLABS, tm, LANES), F32)],
        compiler_params=pltpu.CompilerParams(dimension_semantics=("arbitrary", "arbitrary"),
                                             vmem_limit_bytes=VMEM_LIMIT),
        name="proj",
    )(x, wts["win"], wts["wq"], wts["wk"], wts["wvt"], wts["gq"], wts["gkv"],
      tabs["csk"], tabs["cosq"], tabs["sinq"])


def _dilated_kernel(q_ref, kc_ref, kp_ref, vc_ref, vp_ref, bias_ref, o_ref, lse_ref,
                    kk_ref, vt_ref, qq_ref, *, nsub, group):
    blk = pl.program_id(2)
    nres = q_ref.shape[0]
    lane = lax.broadcasted_iota(jnp.int32, (1, A_WIDTH), 1) % LANES
    lo = lane < A_HEAD_DIM
    for s in range(nres):
        for c in range(nsub):
            q = q_ref[s, c * BAND:(c + 1) * BAND, :]
            qq_ref[s, c, 0:BAND, :] = jnp.where(lo, q, jnp.zeros_like(q))
            qq_ref[s, c, BAND:, :] = jnp.where(lo, jnp.zeros_like(q), q)
        kk_ref[s, 0:BAND, :] = kp_ref[s]
        kk_ref[s, BAND:, :] = kc_ref[s]
        vt_ref[s, :, 0:BAND] = vp_ref[s].T
        vt_ref[s, :, BAND:] = vc_ref[s].T
    top = lax.broadcasted_iota(jnp.int32, (LANES, 1), 0) < A_HEAD_DIM
    first = (blk == 0).astype(jnp.int32)
    pair_cols = [slice(j * LANES, (j + 1) * LANES) for j in range(HEAD_PAIRS)]
    subs = [(s, c) for s in range(nres) for c in range(nsub)]
    for g0 in range(0, len(subs), group):
        blocks = [(s, c, j, slice(c * BAND, (c + 2) * BAND), pair_cols[j])
                  for s, c in subs[g0:g0 + group] for j in range(HEAD_PAIRS)]
        sts = [_dot_nt(kk_ref[s, keys_rows, cols], qq_ref[s, c, :, cols]) + bias_ref[first if c == 0 else 0, j]
               for s, c, j, keys_rows, cols in blocks]
        ms = [jnp.max(st, axis=0, keepdims=True) for st in sts]
        ps = [jnp.exp2(st - m).astype(BF16) for st, m in zip(sts, ms)]
        ones = jnp.ones((ONES_ROWS, 2 * BAND), BF16)
        ot2s = [_dot(jnp.concatenate([vt_ref[s, cols, keys_rows], ones], axis=0), p)
                for (s, _, _, keys_rows, cols), p in zip(blocks, ps)]
        lses = {}
        for (s, c, j, _, cols), ot2, m in zip(blocks, ot2s, ms):
            l = ot2[LANES:LANES + 1, :]
            rl = 1.0 / l
            ot = jnp.where(top, ot2[0:LANES, 0:BAND] * rl[:, 0:BAND], ot2[0:LANES, BAND:] * rl[:, BAND:])
            o_ref[s, c * BAND:(c + 1) * BAND, cols] = ot.T.astype(BF16)
            lse = m + jnp.log2(l)
            lses.setdefault((s, c), []).extend([lse[:, 0:BAND], lse[:, BAND:]])
        for (s, c), rows in lses.items():
            lse_ref[s, :, c * BAND:(c + 1) * BAND] = jnp.concatenate(rows, axis=0)


def _dilated_bias(dilation):
    qi = np.arange(BAND)[None, :]
    kj = np.arange(2 * BAND)[:, None]
    rel = qi + BAND - kj
    valid = (rel >= 0) & (rel <= BAND)
    slopes = 2.0 ** (-8.0 * np.arange(1, A_HEADS + 1, dtype=np.float64) / A_HEADS)
    bias = -slopes[:, None, None] * (rel * dilation).astype(np.float64)[None] * LOG2E
    gen = np.where(valid[None], bias, NEG)
    fst = np.where((valid & (kj >= BAND))[None], bias, NEG)
    pair = lambda t: np.concatenate([t[0::2], t[1::2]], axis=-1)
    return jnp.asarray(np.stack([pair(gen), pair(fst)]), F32)


def _dilated(aqkv, tq):
    bsz, dilation, n, _ = aqkv.shape
    nsub = tq // BAND
    nres = min(dilation, DILATED_GROUP // nsub)
    cur = lambda which: pl.BlockSpec((None, nres, tq, A_WIDTH), lambda b, r, i: (b, r, i, which))
    prev = lambda which: pl.BlockSpec((None, nres, BAND, A_WIDTH),
                                      lambda b, r, i: (b, r, jnp.maximum(i * nsub - 1, 0), which))
    bias = _dilated_bias(dilation)
    return pl.pallas_call(
        functools.partial(_dilated_kernel, nsub=nsub, group=DILATED_GROUP),
        grid=(bsz, dilation // nres, n // tq),
        in_specs=[cur(0), cur(1), prev(1), cur(2), prev(2),
                  pl.BlockSpec(bias.shape, lambda b, r, i: (0, 0, 0, 0))],
        out_specs=[pl.BlockSpec((None, nres, tq, A_WIDTH), lambda b, r, i: (b, r, i, 0)),
                   pl.BlockSpec((None, nres, A_HEADS, tq), lambda b, r, i: (b, r, 0, i))],
        out_shape=[jax.ShapeDtypeStruct((bsz, dilation, n, A_WIDTH), BF16),
                   jax.ShapeDtypeStruct((bsz, dilation, A_HEADS, n), F32)],
        scratch_shapes=[pltpu.VMEM((nres, tq + BAND, A_WIDTH), BF16), pltpu.VMEM((nres, A_WIDTH, tq + BAND), BF16),
                        pltpu.VMEM((nres, nsub, 2 * BAND, A_WIDTH), BF16)],
        compiler_params=pltpu.CompilerParams(dimension_semantics=("arbitrary",) * 3,
                                             vmem_limit_bytes=VMEM_LIMIT),
        name=f"dilated_d{dilation}",
    )(aqkv, aqkv, aqkv, aqkv, aqkv, bias)


def _mla_kernel(qi_ref, kj_ref, q_ref, k_ref, vt_ref, g_ref, o_ref, m_ref, acc_ref, *, npair):
    t = pl.program_id(2)
    qi, kj = qi_ref[t], kj_ref[t]
    heads = range(2 * npair)
    acc_rows = lambda h: slice(h * ACC_ROWS, (h + 1) * ACC_ROWS)

    @pl.when(kj == 0)
    def _():
        m_ref[...] = jnp.full(m_ref.shape, NEG, F32)
        acc_ref[...] = jnp.zeros(acc_ref.shape, F32)

    half = q_ref.shape[0] // 2

    def scores(h, masked):
        cols = slice(h * LANES, (h + 1) * LANES)
        if not masked:
            return (_dot_nt(k_ref[:, cols], q_ref[:, cols]),)
        tri = (lax.broadcasted_iota(jnp.int32, (half, half), 0) <= lax.broadcasted_iota(jnp.int32, (half, half), 1))
        top = _dot_nt(k_ref[0:half, cols], q_ref[:, cols])
        top = jnp.concatenate([jnp.where(tri, top[:, 0:half], NEG), top[:, half:]], axis=1)
        bot = jnp.where(tri, _dot_nt(k_ref[half:, cols], q_ref[half:, cols]), NEG)
        return top, bot

    def accumulate(h, sts):
        v_ext = lambda keys: jnp.concatenate([vt_ref[h * B_V_DIM:(h + 1) * B_V_DIM, keys],
                                              jnp.ones((ONES_ROWS, keys.stop - keys.start), BF16)], axis=0)
        m_old = m_ref[h]
        if len(sts) == 1:
            (st,) = sts
            m_new = jnp.maximum(m_old, jnp.max(st, axis=0, keepdims=True))
            pv = _dot(v_ext(slice(0, 2 * half)), jnp.exp2(st - m_new).astype(BF16))
        else:
            top, bot = sts
            m_top = jnp.max(top, axis=0, keepdims=True)
            m_cur = jnp.concatenate([m_top[:, 0:half], jnp.maximum(m_top[:, half:], jnp.max(bot, axis=0, keepdims=True))],
                                    axis=1)
            m_new = jnp.maximum(m_old, m_cur)
            pv = _dot(v_ext(slice(0, half)), jnp.exp2(top - m_new).astype(BF16))
            pv_bot = _dot(v_ext(slice(half, 2 * half)), jnp.exp2(bot - m_new[:, half:]).astype(BF16))
            pv = jnp.concatenate([pv[:, 0:half], pv[:, half:] + pv_bot], axis=1)
        alpha = jnp.exp2(m_old - m_new)
        m_ref[h] = m_new
        acc_ref[acc_rows(h), :] = alpha * acc_ref[acc_rows(h), :] + pv

    def step(masked):
        pending = {}
        for i in range(len(heads) + MLA_LOOKAHEAD):
            if i < len(heads):
                pending[i] = scores(heads[i], masked)
            if i >= MLA_LOOKAHEAD:
                accumulate(heads[i - MLA_LOOKAHEAD], pending.pop(i - MLA_LOOKAHEAD))

    @pl.when(kj < qi)
    def _():
        step(False)

    @pl.when(kj == qi)
    def _():
        step(True)
        ot = jnp.concatenate([acc_ref[h * ACC_ROWS:h * ACC_ROWS + B_V_DIM, :]
                              * (1.0 / acc_ref[h * ACC_ROWS + B_V_DIM:h * ACC_ROWS + B_V_DIM + 1, :])
                              for h in heads], axis=0)
        o_ref[...] = (ot.T * g_ref[...].astype(F32)).astype(BF16)


def _mla(qf, kf, vt, gates, tq, npair):
    bsz, seq, _ = qf.shape
    nq = seq // tq
    ngrp = HEAD_PAIRS // npair
    qi = np.concatenate([np.full(i + 1, i) for i in range(nq)]).astype(np.int32)
    kj = np.concatenate([np.arange(i + 1) for i in range(nq)]).astype(np.int32)
    wide, narrow = 2 * npair * LANES, npair * LANES
    grid_spec = pltpu.PrefetchScalarGridSpec(
        num_scalar_prefetch=2,
        grid=(bsz, ngrp, len(qi)),
        in_specs=[pl.BlockSpec((None, tq, wide), lambda b, p, t, qi, kj: (b, qi[t], p)),
                  pl.BlockSpec((None, tq, wide), lambda b, p, t, qi, kj: (b, kj[t], p)),
                  pl.BlockSpec((None, narrow, tq), lambda b, p, t, qi, kj: (b, p, kj[t])),
                  pl.BlockSpec((None, tq, narrow), lambda b, p, t, qi, kj: (b, qi[t], ngrp + p))],
        out_specs=pl.BlockSpec((None, tq, narrow), lambda b, p, t, qi, kj: (b, qi[t], p)),
        scratch_shapes=[pltpu.VMEM((2 * npair, 1, tq), F32), pltpu.VMEM((2 * npair * ACC_ROWS, tq), F32)],
    )
    return pl.pallas_call(
        functools.partial(_mla_kernel, npair=npair),
        grid_spec=grid_spec,
        out_shape=jax.ShapeDtypeStruct((bsz, seq, B_WIDTH), BF16),
        compiler_params=pltpu.CompilerParams(dimension_semantics=("arbitrary",) * 3,
                                             vmem_limit_bytes=VMEM_LIMIT),
        name="mla",
    )(jnp.asarray(qi), jnp.asarray(kj), qf, kf, vt, gates)


def _out_kernel(o1_ref, o4_ref, o16_ref, l1_ref, l4_ref, l16_ref, gate_ref, b_ref, x_ref,
                wo_ref, expand_ref, lng_ref, lnb_ref, out_ref, slab4_ref, slab16_ref):
    tm = out_ref.shape[0]
    nslab = A_WIDTH // LANES

    def unperm(o_ref, slab_ref):
        d = o_ref.shape[0]
        for r in range(d):
            o = o_ref[r].astype(F32)
            for sl in range(nslab):
                slab_ref[sl, pl.ds(r, tm // d, stride=d), :] = o[:, sl * LANES:(sl + 1) * LANES]

    unperm(o4_ref, slab4_ref)
    unperm(o16_ref, slab16_ref)
    l1, l4, l16 = l1_ref[...], l4_ref[...], l16_ref[...]
    m = jnp.maximum(jnp.maximum(l1, l4), l16)
    e1, e4, e16 = jnp.exp2(l1 - m), jnp.exp2(l4 - m), jnp.exp2(l16 - m)
    rden = 1.0 / (e1 + e4 + e16)
    pad = jnp.zeros((2 * SUBLANES - A_HEADS, tm), F32)
    wts = [jnp.concatenate([e * rden, pad], axis=0).astype(BF16) for e in (e1, e4, e16)]

    def merge(rows):
        expand = lambda wt: lax.dot_general(wt[:, rows], expand_ref[...], TN_DIMS, preferred_element_type=F32)
        slab = lambda ref: jnp.concatenate([ref[sl, rows, :] for sl in range(nslab)], axis=1)
        a = expand(wts[0]) * o1_ref[rows, :].astype(F32)
        a = a + expand(wts[1]) * slab(slab4_ref) + expand(wts[2]) * slab(slab16_ref)
        return (a * gate_ref[rows, 0:A_WIDTH].astype(F32)).astype(BF16)

    def project(rows, a):
        return _dot(a, wo_ref[0:A_WIDTH, :]) + _dot(b_ref[rows, :], wo_ref[A_WIDTH:, :])

    def norm(rows, y):
        z = DEEPNORM_ALPHA * x_ref[rows, :] + y
        mu = jnp.mean(z, axis=-1, keepdims=True)
        zc = z - mu
        var = jnp.mean(zc * zc, axis=-1, keepdims=True)
        out_ref[rows, :] = zc * lax.rsqrt(var + LN_EPS) * lng_ref[...] + lnb_ref[...]

    chunks = [slice(k * OUT_CHUNK, (k + 1) * OUT_CHUNK) for k in range(tm // OUT_CHUNK)]
    merged, projected = {}, {}
    for k in range(len(chunks) + 2):
        if k < len(chunks):
            merged[k] = merge(chunks[k])
        if 1 <= k <= len(chunks):
            projected[k - 1] = project(chunks[k - 1], merged.pop(k - 1))
        if k >= 2:
            norm(chunks[k - 2], projected.pop(k - 2))


def _finish(os_, lses, gates, b_out, x, wo, expand, ln_g, ln_b, tm):
    bsz, seq, _ = x.shape
    row = lambda w: pl.BlockSpec((None, tm, w), lambda b, i: (b, i, 0))
    res = lambda d: pl.BlockSpec((None, d, tm // d, A_WIDTH), lambda b, i: (b, 0, i, 0))
    lse = pl.BlockSpec((None, A_HEADS, tm), lambda b, i: (b, 0, i))
    full = lambda a: pl.BlockSpec(a.shape, lambda b, i: (0,) * a.ndim)
    return pl.pallas_call(
        _out_kernel,
        grid=(bsz, seq // tm),
        in_specs=[row(A_WIDTH), res(DILATIONS[1]), res(DILATIONS[2]),
                  lse, lse, lse, row(A_WIDTH), row(B_WIDTH), row(D_MODEL),
                  full(wo), full(expand), full(ln_g), full(ln_b)],
        out_specs=row(D_MODEL),
        out_shape=jax.ShapeDtypeStruct((bsz, seq, D_MODEL), F32),
        scratch_shapes=[pltpu.VMEM((A_WIDTH // LANES, tm, LANES), F32)] * 2,
        compiler_params=pltpu.CompilerParams(dimension_semantics=("arbitrary", "arbitrary"),
                                             vmem_limit_bytes=VMEM_LIMIT),
        name="merge_out_ln",
    )(*os_, *lses, gates, b_out, x, wo, expand, ln_g, ln_b)


def _rot_cols(w):
    half = w.shape[-1] // 2
    return jnp.concatenate([-w[..., half:], w[..., :half]], axis=-1)


def _layout_weights(w_in, q_norm_g, w_uq, kv_norm_g, w_ukv, w_o):
    aw = A_WIDTH
    a_q, a_k, a_v, a_gate = (w_in[:, i * aw:(i + 1) * aw] for i in range(4))
    o = 4 * aw
    c_q, o = w_in[:, o:o + Q_LORA_RANK], o + Q_LORA_RANK
    c_kv, o = w_in[:, o:o + KV_LORA_RANK], o + KV_LORA_RANK
    k_rope, o = w_in[:, o:o + B_ROPE_DIM], o + B_ROPE_DIM
    b_gate = w_in[:, o:o + B_WIDTH]
    pad = jnp.zeros((D_MODEL, LANES - 2 * B_ROPE_DIM), w_in.dtype)
    win = jnp.concatenate([a_q * (A_HEAD_DIM ** -0.5 * LOG2E), a_k, a_v, a_gate, b_gate, c_q, c_kv,
                           k_rope, _rot_cols(k_rope), pad], axis=1).astype(BF16)
    uq = w_uq.reshape(Q_LORA_RANK, B_HEADS, B_NOPE_DIM + B_ROPE_DIM)
    nope, rope = uq[..., :B_NOPE_DIM], uq[..., B_NOPE_DIM:]
    z = lambda w: jnp.zeros((Q_LORA_RANK, B_HEADS, w), w_uq.dtype)
    tail = LANES - B_NOPE_DIM - B_ROPE_DIM
    w1 = jnp.concatenate([nope, rope, z(tail)], axis=-1).reshape(Q_LORA_RANK, B_HEADS * LANES)
    w2 = jnp.concatenate([z(B_NOPE_DIM), _rot_cols(rope), z(tail)], axis=-1).reshape(Q_LORA_RANK, B_HEADS * LANES)
    wq = jnp.concatenate([w1, w2], axis=1).astype(BF16)
    ukv = w_ukv.reshape(KV_LORA_RANK, B_HEADS, B_NOPE_DIM + B_V_DIM)
    wk = jnp.concatenate([ukv[..., :B_NOPE_DIM], jnp.zeros((KV_LORA_RANK, B_HEADS, LANES - B_NOPE_DIM), w_ukv.dtype)],
                         axis=-1).reshape(KV_LORA_RANK, B_HEADS * LANES)
    wvt = ukv[..., B_NOPE_DIM:].reshape(KV_LORA_RANK, B_WIDTH).T
    expand =np.zeros((2 * SUBLANES, A_WIDTH), np.float32)
    for h in range(A_HEADS):
        expand[h, h * A_HEAD_DIM:(h + 1) * A_HEAD_DIM] = 1.0
    return {"win": win, "wq": wq, "wk": wk.astype(BF16), "wvt": wvt.astype(BF16),
            "gq": q_norm_g.reshape(1, -1).astype(F32), "gkv": kv_norm_g.reshape(1, -1).astype(F32),
            "wo": w_o.astype(BF16), "expand": jnp.asarray(expand, BF16)}


def _rope_tables(seq):
    inv_freq = ROPE_THETA ** (-jnp.arange(0, B_ROPE_DIM, 2, dtype=F32) / B_ROPE_DIM)
    ang = jnp.arange(seq, dtype=F32)[:, None] * inv_freq[None, :]
    ang = jnp.concatenate([ang, ang], axis=-1)
    cos, sin = jnp.cos(ang), jnp.sin(ang)
    scale = (B_NOPE_DIM + B_ROPE_DIM) ** -0.5 * LOG2E
    one = jnp.ones((seq, B_NOPE_DIM), F32)
    zn = jnp.zeros((seq, B_NOPE_DIM), F32)
    zt = jnp.zeros((seq, LANES - B_NOPE_DIM - B_ROPE_DIM), F32)
    cosq = jnp.tile(jnp.concatenate([one, cos, zt], axis=1) * scale, (1, B_HEADS))
    sinq = jnp.tile(jnp.concatenate([zn, sin, zt], axis=1) * scale, (1, B_HEADS))
    csk = jnp.concatenate([cos, sin, jnp.zeros((seq, LANES - 2 * B_ROPE_DIM), F32)], axis=1)
    return {"csk": csk, "cosq": cosq, "sinq": sinq}


def kernel(x, w_in, q_norm_g, w_uq, kv_norm_g, w_ukv, w_o, ln_g, ln_b):
    bsz, seq, _ = x.shape
    wts = _layout_weights(w_in, q_norm_g, w_uq, kv_norm_g, w_ukv, w_o)
    tabs = _rope_tables(seq)
    aqkv1, aqkv4, aqkv16, gates, qf, kf, vt = _project(x, wts, tabs, tm=512)
    os_, lses = [], []
    for aqkv in (aqkv1[:, None], aqkv4, aqkv16):
        d, n = aqkv.shape[1], aqkv.shape[2]
        o, lse_t = _dilated(aqkv, tq=min(512, n))
        os_.append(o)
        lses.append(jnp.transpose(lse_t, (0, 2, 3, 1)).reshape(bsz, A_HEADS, seq))
    os_[0] = os_[0][:, 0]
    b_out = _mla(qf, kf, vt, gates, tq=512, npair=4)
    return _finish(os_, lses, gates, b_out, x, wts["wo"], wts["expand"],
                   ln_g.reshape(1, -1).astype(F32), ln_b.reshape(1, -1).astype(F32), tm=512)
```

```python
import functools

import jax
import jax.numpy as jnp
import numpy as np
from jax import lax
from jax.experimental import pallas as pl
from jax.experimental.pallas import tpu as pltpu

D_MODEL = 1024
A_HEADS = 8
A_HEAD_DIM = 64
A_WIDTH = A_HEADS * A_HEAD_DIM
DILATED_PATTERNS = ((128, 1), (512, 4), (2048, 16))
DILATIONS = tuple(d for _, d in DILATED_PATTERNS)
BAND = 128
B_HEADS = 8
B_NOPE_DIM = 64
B_ROPE_DIM = 32
B_V_DIM = 64
B_WIDTH = B_HEADS * B_V_DIM
Q_LORA_RANK = 256
KV_LORA_RANK = 128
ROPE_THETA = 10000.0
LN_EPS = 1e-5
RMS_EPS = 1e-6
DEPTH = 1
DEEPNORM_ALPHA = (2 * DEPTH) ** 0.25

LANES = 128
SUBLANES = 8
HEAD_PAIRS = A_HEADS // 2
NEG = -1e30
VMEM_LIMIT = 56 * 1024 * 1024

BF16 = jnp.bfloat16
F32 = jnp.float32

C_AQKV = 0
C_GATE = 1536
C_CQ = 2560
C_CKV = 2816
C_KR = 2944
IN_COLS = 3072
AQKV_SLABS = 3 * A_WIDTH // LANES

TN_DIMS = (((0,), (0,)), ((), ()))
LOG2E = float(np.log2(np.e))
ONES_ROWS = 16
ACC_ROWS = B_V_DIM + ONES_ROWS
MLA_LOOKAHEAD = 2
OUT_CHUNK = 128
DILATED_GROUP = 4


def _dot(a, b):
    return jnp.dot(a, b, preferred_element_type=F32)


def _dot_nt(a, b):
    return lax.dot_general(a, b, (((1,), (1,)), ((), ())), preferred_element_type=F32)


def _rms(t, g):
    return t * lax.rsqrt(jnp.mean(t * t, axis=-1, keepdims=True) + RMS_EPS) * g


def _silu(t):
    return t / (1.0 + jnp.exp(-t))


def _proj_kernel(x_ref, win_ref, wq_ref, wk_ref, wvt_ref, gq_ref, gkv_ref, csk_ref, cosq_ref, sinq_ref,
                 aqkv1_ref, aqkv4_ref, aqkv16_ref, gate_ref, qf_ref, kf_ref, vt_ref, slab_ref):
    xb = x_ref[...].astype(BF16)
    tm = xb.shape[0]
    lane_tile = lax.broadcasted_iota(jnp.int32, (1, LANES), 1)
    aqkv = _dot(xb, win_ref[:, C_AQKV:C_GATE])
    aqkv1_ref[...] = aqkv.astype(BF16)
    for sl in range(AQKV_SLABS):
        slab_ref[sl] = aqkv[:, sl * LANES:(sl + 1) * LANES]
    for d, ref in ((DILATIONS[1], aqkv4_ref), (DILATIONS[2], aqkv16_ref)):
        for r in range(d):
            for sl in range(AQKV_SLABS):
                ref[r, :, sl * LANES:(sl + 1) * LANES] = slab_ref[sl, pl.ds(r, tm // d, stride=d), :].astype(BF16)
    gate_ref[...] = _silu(_dot(xb, win_ref[:, C_GATE:C_CQ])).astype(BF16)
    lat = _dot(xb, win_ref[:, C_CQ:IN_COLS])
    cq = lat[:, 0:Q_LORA_RANK]
    ckv = lat[:, Q_LORA_RANK:Q_LORA_RANK + KV_LORA_RANK]
    rope = lat[:, Q_LORA_RANK + KV_LORA_RANK:]
    q2 = _dot(_rms(cq, gq_ref[...]).astype(BF16), wq_ref[...])
    qf_ref[...] = (q2[:, :B_HEADS * LANES] * cosq_ref[...] + q2[:, B_HEADS * LANES:] * sinq_ref[...]).astype(BF16)
    r2 = rope * csk_ref[...]
    kr = pltpu.roll(r2, B_NOPE_DIM, axis=1) + pltpu.roll(r2, B_NOPE_DIM - B_ROPE_DIM, axis=1)
    kr = jnp.where((lane_tile >= B_NOPE_DIM) & (lane_tile < B_NOPE_DIM + B_ROPE_DIM), kr, 0.0)
    kvn = _rms(ckv, gkv_ref[...]).astype(BF16)
    kn = _dot(kvn, wk_ref[...])
    for h in range(B_HEADS):
        kf_ref[:, h * LANES:(h + 1) * LANES] = (kn[:, h * LANES:(h + 1) * LANES] + kr).astype(BF16)
    vt_ref[...] = _dot_nt(wvt_ref[...], kvn).astype(BF16)


def _project(x, wts, tabs, tm):
    bsz, seq, _ = x.shape
    row = lambda w: pl.BlockSpec((None, tm, w), lambda i, b: (b, i, 0))
    res = lambda d: pl.BlockSpec((None, d, tm // d, 3 * A_WIDTH), lambda i, b: (b, 0, i, 0))
    full = lambda a: pl.BlockSpec(a.shape, lambda i, b: (0,) * a.ndim)
    tab = lambda w: pl.BlockSpec((tm, w), lambda i, b: (i, 0))
    out = lambda w: jax.ShapeDtypeStruct((bsz, seq, w), BF16)
    res_out = lambda d: jax.ShapeDtypeStruct((bsz, d, seq // d, 3 * A_WIDTH), BF16)
    return pl.pallas_call(
        _proj_kernel,
        grid=(seq // tm, bsz),
        in_specs=[row(D_MODEL), full(wts["win"]), full(wts["wq"]), full(wts["wk"]), full(wts["wvt"]),
                  full(wts["gq"]), full(wts["gkv"]),
                  tab(LANES), tab(B_HEADS * LANES), tab(B_HEADS * LANES)],
        out_specs=[row(3 * A_WIDTH), res(DILATIONS[1]), res(DILATIONS[2]), row(A_WIDTH + B_WIDTH),
                   row(B_HEADS * LANES), row(B_HEADS * LANES),
                   pl.BlockSpec((None, B_WIDTH, tm), lambda i, b: (b, 0, i))],
        out_shape=[out(3 * A_WIDTH), res_out(DILATIONS[1]), res_out(DILATIONS[2]), out(A_WIDTH + B_WIDTH),
                   out(B_HEADS * LANES), out(B_HEADS * LANES),
                   jax.ShapeDtypeStruct((bsz, B_WIDTH, seq), BF16)],
        scratch_shapes=[pltpu.VMEM((AQKV_SLABS, tm, LANES), F32)],
        compiler_params=pltpu.CompilerParams(dimension_semantics=("arbitrary", "arbitrary"),
                                             vmem_limit_bytes=VMEM_LIMIT),
        name="proj",
    )(x, wts["win"], wts["wq"], wts["wk"], wts["wvt"], wts["gq"], wts["gkv"],
      tabs["csk"], tabs["cosq"], tabs["sinq"])


def _dilated_kernel(q_ref, kc_ref, kp_ref, vc_ref, vp_ref, bias_ref, o_ref, lse_ref,
                    kk_ref, vt_ref, qq_ref, *, nsub, group):
    blk = pl.program_id(2)
    nres = q_ref.shape[0]
    lane = lax.broadcasted_iota(jnp.int32, (1, A_WIDTH), 1) % LANES
    lo = lane < A_HEAD_DIM
    for s in range(nres):
        for c in range(nsub):
            q = q_ref[s, c * BAND:(c + 1) * BAND, :]
            qq_ref[s, c, 0:BAND, :] = jnp.where(lo, q, jnp.zeros_like(q))
            qq_ref[s, c, BAND:, :] = jnp.where(lo, jnp.zeros_like(q), q)
        kk_ref[s, 0:BAND, :] = kp_ref[s]
        kk_ref[s, BAND:, :] = kc_ref[s]
        vt_ref[s, :, 0:BAND] = vp_ref[s].T
        vt_ref[s, :, BAND:] = vc_ref[s].T
    top = lax.broadcasted_iota(jnp.int32, (LANES, 1), 0) < A_HEAD_DIM
    first = (blk == 0).astype(jnp.int32)
    pair_cols = [slice(j * LANES, (j + 1) * LANES) for j in range(HEAD_PAIRS)]
    subs = [(s, c) for s in range(nres) for c in range(nsub)]
    for g0 in range(0, len(subs), group):
        blocks = [(s, c, j, slice(c * BAND, (c + 2) * BAND), pair_cols[j])
                  for s, c in subs[g0:g0 + group] for j in range(HEAD_PAIRS)]
        sts = [_dot_nt(kk_ref[s, keys_rows, cols], qq_ref[s, c, :, cols]) + bias_ref[first if c == 0 else 0, j]
               for s, c, j, keys_rows, cols in blocks]
        ms = [jnp.max(st, axis=0, keepdims=True) for st in sts]
        ps = [jnp.exp2(st - m).astype(BF16) for st, m in zip(sts, ms)]
        ones = jnp.ones((ONES_ROWS, 2 * BAND), BF16)
        ot2s = [_dot(jnp.concatenate([vt_ref[s, cols, keys_rows], ones], axis=0), p)
                for (s, _, _, keys_rows, cols), p in zip(blocks, ps)]
        lses = {}
        for (s, c, j, _, cols), ot2, m in zip(blocks, ot2s, ms):
            l = ot2[LANES:LANES + 1, :]
            rl = 1.0 / l
            ot = jnp.where(top, ot2[0:LANES, 0:BAND] * rl[:, 0:BAND], ot2[0:LANES, BAND:] * rl[:, BAND:])
            o_ref[s, c * BAND:(c + 1) * BAND, cols] = ot.T.astype(BF16)
            lse = m + jnp.log2(l)
            lses.setdefault((s, c), []).extend([lse[:, 0:BAND], lse[:, BAND:]])
        for (s, c), rows in lses.items():
            lse_ref[s, :, c * BAND:(c + 1) * BAND] = jnp.concatenate(rows, axis=0)


def _dilated_bias(dilation):
    qi = np.arange(BAND)[None, :]
    kj = np.arange(2 * BAND)[:, None]
    rel = qi + BAND - kj
    valid = (rel >= 0) & (rel <= BAND)
    slopes = 2.0 ** (-8.0 * np.arange(1, A_HEADS + 1, dtype=np.float64) / A_HEADS)
    bias = -slopes[:, None, None] * (rel * dilation).astype(np.float64)[None] * LOG2E
    gen = np.where(valid[None], bias, NEG)
    fst = np.where((valid & (kj >= BAND))[None], bias, NEG)
    pair = lambda t: np.concatenate([t[0::2], t[1::2]], axis=-1)
    return jnp.asarray(np.stack([pair(gen), pair(fst)]), F32)


def _dilated(aqkv, tq):
    bsz, dilation, n, _ = aqkv.shape
    nsub = tq // BAND
    nres = min(dilation, DILATED_GROUP // nsub)
    cur = lambda which: pl.BlockSpec((None, nres, tq, A_WIDTH), lambda b, r, i: (b, r, i, which))
    prev = lambda which: pl.BlockSpec((None, nres, BAND, A_WIDTH),
                                      lambda b, r, i: (b, r, jnp.maximum(i * nsub - 1, 0), which))
    bias = _dilated_bias(dilation)
    return pl.pallas_call(
        functools.partial(_dilated_kernel, nsub=nsub, group=DILATED_GROUP),
        grid=(bsz, dilation // nres, n // tq),
        in_specs=[cur(0), cur(1), prev(1), cur(2), prev(2),
                  pl.BlockSpec(bias.shape, lambda b, r, i: (0, 0, 0, 0))],
        out_specs=[pl.BlockSpec((None, nres, tq, A_WIDTH), lambda b, r, i: (b, r, i, 0)),
                   pl.BlockSpec((None, nres, A_HEADS, tq), lambda b, r, i: (b, r, 0, i))],
        out_shape=[jax.ShapeDtypeStruct((bsz, dilation, n, A_WIDTH), BF16),
                   jax.ShapeDtypeStruct((bsz, dilation, A_HEADS, n), F32)],
        scratch_shapes=[pltpu.VMEM((nres, tq + BAND, A_WIDTH), BF16), pltpu.VMEM((nres, A_WIDTH, tq + BAND), BF16),
                        pltpu.VMEM((nres, nsub, 2 * BAND, A_WIDTH), BF16)],
        compiler_params=pltpu.CompilerParams(dimension_semantics=("arbitrary",) * 3,
                                             vmem_limit_bytes=VMEM_LIMIT),
        name=f"dilated_d{dilation}",
    )(aqkv, aqkv, aqkv, aqkv, aqkv, bias)


def _mla_kernel(qi_ref, kj_ref, q_ref, k_ref, vt_ref, g_ref, o_ref, m_ref, acc_ref, *, npair):
    t = pl.program_id(2)
    qi, kj = qi_ref[t], kj_ref[t]
    heads = range(2 * npair)
    acc_rows = lambda h: slice(h * ACC_ROWS, (h + 1) * ACC_ROWS)

    @pl.when(kj == 0)
    def _():
        m_ref[...] = jnp.full(m_ref.shape, NEG, F32)
        acc_ref[...] = jnp.zeros(acc_ref.shape, F32)

    half = q_ref.shape[0] // 2

    def scores(h, masked):
        cols = slice(h * LANES, (h + 1) * LANES)
        if not masked:
            return (_dot_nt(k_ref[:, cols], q_ref[:, cols]).astype(BF16),)
        tri = (lax.broadcasted_iota(jnp.int32, (half, half), 0) <= lax.broadcasted_iota(jnp.int32, (half, half), 1))
        top = _dot_nt(k_ref[0:half, cols], q_ref[:, cols])
        top = jnp.concatenate([jnp.where(tri, top[:, 0:half], NEG), top[:, half:]], axis=1)
        bot = jnp.where(tri, _dot_nt(k_ref[half:, cols], q_ref[half:, cols]), NEG)
        return top.astype(BF16), bot.astype(BF16)

    def accumulate(h, sts):
        v_ext = lambda keys: jnp.concatenate([vt_ref[h * B_V_DIM:(h + 1) * B_V_DIM, keys],
                                              jnp.ones((ONES_ROWS, keys.stop - keys.start), BF16)], axis=0)
        m_old = m_ref[h]
        colmax = lambda t: jnp.max(t, axis=0, keepdims=True).astype(F32)
        if len(sts) == 1:
            (st,) = sts
            m_new = jnp.maximum(m_old, colmax(st))
            pv = _dot(v_ext(slice(0, 2 * half)), jnp.exp2(st - m_new.astype(BF16)))
        else:
            top, bot = sts
            m_top = colmax(top)
            m_cur = jnp.concatenate([m_top[:, 0:half], jnp.maximum(m_top[:, half:], colmax(bot))], axis=1)
            m_new = jnp.maximum(m_old, m_cur)
            m_b = m_new.astype(BF16)
            pv = _dot(v_ext(slice(0, half)), jnp.exp2(top - m_b))
            pv_bot = _dot(v_ext(slice(half, 2 * half)), jnp.exp2(bot - m_b[:, half:]))
            pv = jnp.concatenate([pv[:, 0:half], pv[:, half:] + pv_bot], axis=1)
        alpha = jnp.exp2(m_old - m_new)
        m_ref[h] = m_new
        acc_ref[acc_rows(h), :] = alpha * acc_ref[acc_rows(h), :] + pv

    def step(masked):
        pending = {}
        for i in range(len(heads) + MLA_LOOKAHEAD):
            if i < len(heads):
                pending[i] = scores(heads[i], masked)
            if i >= MLA_LOOKAHEAD:
                accumulate(heads[i - MLA_LOOKAHEAD], pending.pop(i - MLA_LOOKAHEAD))

    @pl.when(kj < qi)
    def _():
        step(False)

    @pl.when(kj == qi)
    def _():
        step(True)
        ot = jnp.concatenate([acc_ref[h * ACC_ROWS:h * ACC_ROWS + B_V_DIM, :]
                              * (1.0 / acc_ref[h * ACC_ROWS + B_V_DIM:h * ACC_ROWS + B_V_DIM + 1, :])
                              for h in heads], axis=0)
        o_ref[...] = (ot.T * g_ref[...].astype(F32)).astype(BF16)


def _mla(qf, kf, vt, gates, tq, npair):
    bsz, seq, _ = qf.shape
    nq = seq // tq
    ngrp = HEAD_PAIRS // npair
    qi = np.concatenate([np.full(i + 1, i) for i in range(nq)]).astype(np.int32)
    kj = np.concatenate([np.arange(i + 1) for i in range(nq)]).astype(np.int32)
    wide, narrow = 2 * npair * LANES, npair * LANES
    grid_spec = pltpu.PrefetchScalarGridSpec(
        num_scalar_prefetch=2,
        grid=(bsz, ngrp, len(qi)),
        in_specs=[pl.BlockSpec((None, tq, wide), lambda b, p, t, qi, kj: (b, qi[t], p)),
                  pl.BlockSpec((None, tq, wide), lambda b, p, t, qi, kj: (b, kj[t], p)),
                  pl.BlockSpec((None, narrow, tq), lambda b, p, t, qi, kj: (b, p, kj[t])),
                  pl.BlockSpec((None, tq, narrow), lambda b, p, t, qi, kj: (b, qi[t], ngrp + p))],
        out_specs=pl.BlockSpec((None, tq, narrow), lambda b, p, t, qi, kj: (b, qi[t], p)),
        scratch_shapes=[pltpu.VMEM((2 * npair, 1, tq), F32), pltpu.VMEM((2 * npair * ACC_ROWS, tq), F32)],
    )
    return pl.pallas_call(
        functools.partial(_mla_kernel, npair=npair),
        grid_spec=grid_spec,
        out_shape=jax.ShapeDtypeStruct((bsz, seq, B_WIDTH), BF16),
        compiler_params=pltpu.CompilerParams(dimension_semantics=("arbitrary",) * 3,
                                             vmem_limit_bytes=VMEM_LIMIT),
        name="mla",
    )(jnp.asarray(qi), jnp.asarray(kj), qf, kf, vt, gates)


def _out_kernel(o1_ref, o4_ref, o16_ref, l1_ref, l4_ref, l16_ref, gate_ref, b_ref, x_ref,
                wo_ref, expand_ref, lng_ref, lnb_ref, out_ref, slab4_ref, slab16_ref):
    tm = out_ref.shape[0]
    nslab = A_WIDTH // LANES

    def unperm(o_ref, slab_ref):
        d = o_ref.shape[0]
        for r in range(d):
            o = o_ref[r].astype(F32)
            for sl in range(nslab):
                slab_ref[sl, pl.ds(r, tm // d, stride=d), :] = o[:, sl * LANES:(sl + 1) * LANES]

    unperm(o4_ref, slab4_ref)
    unperm(o16_ref, slab16_ref)
    l1, l4, l16 = l1_ref[...], l4_ref[...], l16_ref[...]
    m = jnp.maximum(jnp.maximum(l1, l4), l16)
    e1, e4, e16 = jnp.exp2(l1 - m), jnp.exp2(l4 - m), jnp.exp2(l16 - m)
    rden = 1.0 / (e1 + e4 + e16)
    pad = jnp.zeros((2 * SUBLANES - A_HEADS, tm), F32)
    wts = [jnp.concatenate([e * rden, pad], axis=0).astype(BF16) for e in (e1, e4, e16)]

    def merge(rows):
        expand = lambda wt: lax.dot_general(wt[:, rows], expand_ref[...], TN_DIMS, preferred_element_type=F32)
        slab = lambda ref: jnp.concatenate([ref[sl, rows, :] for sl in range(nslab)], axis=1)
        a = expand(wts[0]) * o1_ref[rows, :].astype(F32)
        a = a + expand(wts[1]) * slab(slab4_ref) + expand(wts[2]) * slab(slab16_ref)
        return (a * gate_ref[rows, 0:A_WIDTH].astype(F32)).astype(BF16)

    def project(rows, a):
        return _dot(a, wo_ref[0:A_WIDTH, :]) + _dot(b_ref[rows, :], wo_ref[A_WIDTH:, :])

    def norm(rows, y):
        z = DEEPNORM_ALPHA * x_ref[rows, :] + y
        mu = jnp.mean(z, axis=-1, keepdims=True)
        zc = z - mu
        var = jnp.mean(zc * zc, axis=-1, keepdims=True)
        out_ref[rows, :] = zc * lax.rsqrt(var + LN_EPS) * lng_ref[...] + lnb_ref[...]

    chunks = [slice(k * OUT_CHUNK, (k + 1) * OUT_CHUNK) for k in range(tm // OUT_CHUNK)]
    merged, projected = {}, {}
    for k in range(len(chunks) + 2):
        if k < len(chunks):
            merged[k] = merge(chunks[k])
        if 1 <= k <= len(chunks):
            projected[k - 1] = project(chunks[k - 1], merged.pop(k - 1))
        if k >= 2:
            norm(chunks[k - 2], projected.pop(k - 2))


def _finish(os_, lses, gates, b_out, x, wo, expand, ln_g, ln_b, tm):
    bsz, seq, _ = x.shape
    row = lambda w: pl.BlockSpec((None, tm, w), lambda b, i: (b, i, 0))
    res = lambda d: pl.BlockSpec((None, d, tm // d, A_WIDTH), lambda b, i: (b, 0, i, 0))
    lse = pl.BlockSpec((None, A_HEADS, tm), lambda b, i: (b, 0, i))
    full = lambda a: pl.BlockSpec(a.shape, lambda b, i: (0,) * a.ndim)
    return pl.pallas_call(
        _out_kernel,
        grid=(bsz, seq // tm),
        in_specs=[row(A_WIDTH), res(DILATIONS[1]), res(DILATIONS[2]),
                  lse, lse, lse, row(A_WIDTH), row(B_WIDTH), row(D_MODEL),
                  full(wo), full(expand), full(ln_g), full(ln_b)],
        out_specs=row(D_MODEL),
        out_shape=jax.ShapeDtypeStruct((bsz, seq, D_MODEL), F32),
        scratch_shapes=[pltpu.VMEM((A_WIDTH // LANES, tm, LANES), F32)] * 2,
        compiler_params=pltpu.CompilerParams(dimension_semantics=("arbitrary", "arbitrary"),
                                             vmem_limit_bytes=VMEM_LIMIT),
        name="merge_out_ln",
    )(*os_, *lses, gates, b_out, x, wo, expand, ln_g, ln_b)


def _rot_cols(w):
    half = w.shape[-1] // 2
    return jnp.concatenate([-w[..., half:], w[..., :half]], axis=-1)


def _layout_weights(w_in, q_norm_g, w_uq, kv_norm_g, w_ukv, w_o):
    aw = A_WIDTH
    a_q, a_k, a_v, a_gate = (w_in[:, i * aw:(i + 1) * aw] for i in range(4))
    o = 4 * aw
    c_q, o = w_in[:, o:o + Q_LORA_RANK], o + Q_LORA_RANK
    c_kv, o = w_in[:, o:o + KV_LORA_RANK], o + KV_LORA_RANK
    k_rope, o = w_in[:, o:o + B_ROPE_DIM], o + B_ROPE_DIM
    b_gate = w_in[:, o:o + B_WIDTH]
    pad = jnp.zeros((D_MODEL, LANES - 2 * B_ROPE_DIM), w_in.dtype)
    win = jnp.concatenate([a_q * (A_HEAD_DIM ** -0.5 * LOG2E), a_k, a_v, a_gate, b_gate, c_q, c_kv,
                           k_rope, _rot_cols(k_rope), pad], axis=1).astype(BF16)
    uq = w_uq.reshape(Q_LORA_RANK, B_HEADS, B_NOPE_DIM + B_ROPE_DIM)
    nope, rope = uq[..., :B_NOPE_DIM], uq[..., B_NOPE_DIM:]
    z = lambda w: jnp.zeros((Q_LORA_RANK, B_HEADS, w), w_uq.dtype)
    tail = LANES - B_NOPE_DIM - B_ROPE_DIM
    w1 = jnp.concatenate([nope, rope, z(tail)], axis=-1).reshape(Q_LORA_RANK, B_HEADS * LANES)
    w2 = jnp.concatenate([z(B_NOPE_DIM), _rot_cols(rope), z(tail)], axis=-1).reshape(Q_LORA_RANK, B_HEADS * LANES)
    wq = jnp.concatenate([w1, w2], axis=1).astype(BF16)
    ukv = w_ukv.reshape(KV_LORA_RANK, B_HEADS, B_NOPE_DIM + B_V_DIM)
    wk = jnp.concatenate([ukv[..., :B_NOPE_DIM], jnp.zeros((KV_LORA_RANK, B_HEADS, LANES - B_NOPE_DIM), w_ukv.dtype)],
                         axis=-1).reshape(KV_LORA_RANK, B_HEADS * LANES)
    wvt = ukv[..., B_NOPE_DIM:].reshape(KV_LORA_RANK, B_WIDTH).T
    expand =np.zeros((2 * SUBLANES, A_WIDTH), np.float32)
    for h in range(A_HEADS):
        expand[h, h * A_HEAD_DIM:(h + 1) * A_HEAD_DIM] = 1.0
    return {"win": win, "wq": wq, "wk": wk.astype(BF16), "wvt": wvt.astype(BF16),
            "gq": q_norm_g.reshape(1, -1).astype(F32), "gkv": kv_norm_g.reshape(1, -1).astype(F32),
            "wo": w_o.astype(BF16), "expand": jnp.asarray(expand, BF16)}


def _rope_tables(seq):
    inv_freq = ROPE_THETA ** (-jnp.arange(0, B_ROPE_DIM, 2, dtype=F32) / B_ROPE_DIM)
    ang = jnp.arange(seq, dtype=F32)[:, None] * inv_freq[None, :]
    ang = jnp.concatenate([ang, ang], axis=-1)
    cos, sin = jnp.cos(ang), jnp.sin(ang)
    scale = (B_NOPE_DIM + B_ROPE_DIM) ** -0.5 * LOG2E
    one = jnp.ones((seq, B_NOPE_DIM), F32)
    zn = jnp.zeros((seq, B_NOPE_DIM), F32)
    zt = jnp.zeros((seq, LANES - B_NOPE_DIM - B_ROPE_DIM), F32)
    cosq = jnp.tile(jnp.concatenate([one, cos, zt], axis=1) * scale, (1, B_HEADS))
    sinq = jnp.tile(jnp.concatenate([zn, sin, zt], axis=1) * scale, (1, B_HEADS))
    csk = jnp.concatenate([cos, sin, jnp.zeros((seq, LANES - 2 * B_ROPE_DIM), F32)], axis=1)
    return {"csk": csk, "cosq": cosq, "sinq": sinq}


def kernel(x, w_in, q_norm_g, w_uq, kv_norm_g, w_ukv, w_o, ln_g, ln_b):
    bsz, seq, _ = x.shape
    wts = _layout_weights(w_in, q_norm_g, w_uq, kv_norm_g, w_ukv, w_o)
    tabs = _rope_tables(seq)
    aqkv1, aqkv4, aqkv16, gates, qf, kf, vt = _project(x, wts, tabs, tm=512)
    os_, lses = [], []
    for aqkv in (aqkv1[:, None], aqkv4, aqkv16):
        d, n = aqkv.shape[1], aqkv.shape[2]
        o, lse_t = _dilated(aqkv, tq=min(512, n))
        os_.append(o)
        lses.append(jnp.transpose(lse_t, (0, 2, 3, 1)).reshape(bsz, A_HEADS, seq))
    os_[0] = os_[0][:, 0]
    b_out = _mla(qf, kf, vt, gates, tq=512, npair=4)
    return _finish(os_, lses, gates, b_out, x, wts["wo"], wts["expand"],
                   ln_g.reshape(1, -1).astype(F32), ln_b.reshape(1, -1).astype(F32), tm=512)
```

```python
import functools

import jax
import jax.numpy as jnp
import numpy as np
from jax import lax
from jax.experimental import pallas as pl
from jax.experimental.pallas import tpu as pltpu

D_MODEL = 1024
A_HEADS = 8
A_HEAD_DIM = 64
A_WIDTH = A_HEADS * A_HEAD_DIM
DILATED_PATTERNS = ((128, 1), (512, 4), (2048, 16))
DILATIONS = tuple(d for _, d in DILATED_PATTERNS)
BAND = 128
B_HEADS = 8
B_NOPE_DIM = 64
B_ROPE_DIM = 32
B_V_DIM = 64
B_WIDTH = B_HEADS * B_V_DIM
Q_LORA_RANK = 256
KV_LORA_RANK = 128
ROPE_THETA = 10000.0
LN_EPS = 1e-5
RMS_EPS = 1e-6
DEPTH = 1
DEEPNORM_ALPHA = (2 * DEPTH) ** 0.25

LANES = 128
SUBLANES = 8
HEAD_PAIRS = A_HEADS // 2
NEG = -1e30
VMEM_LIMIT = 56 * 1024 * 1024

BF16 = jnp.bfloat16
F32 = jnp.float32

C_AQKV = 0
C_GATE = 1536
C_CQ = 2560
C_CKV = 2816
C_KR = 2944
IN_COLS = 3072
AQKV_SLABS = 3 * A_WIDTH // LANES

TN_DIMS = (((0,), (0,)), ((), ()))
LOG2E = float(np.log2(np.e))
ONES_ROWS = 16
ACC_ROWS = B_V_DIM + ONES_ROWS
MLA_LOOKAHEAD = 2
OUT_CHUNK = 128
DILATED_GROUP = 8
DILATED_TQ = DILATED_GROUP * BAND


def _dot(a, b):
    return jnp.dot(a, b, preferred_element_type=F32)


def _dot_nt(a, b):
    return lax.dot_general(a, b, (((1,), (1,)), ((), ())), preferred_element_type=F32)


def _rms(t, g):
    return t * lax.rsqrt(jnp.mean(t * t, axis=-1, keepdims=True) + RMS_EPS) * g


def _silu(t):
    return t / (1.0 + jnp.exp(-t))


def _proj_kernel(x_ref, win_ref, wq_ref, wk_ref, wvt_ref, gq_ref, gkv_ref, csk_ref, cosq_ref, sinq_ref,
                 aqkv1_ref, aqkv4_ref, aqkv16_ref, gate_ref, qf_ref, kf_ref, vt_ref, slab_ref):
    xb = x_ref[...].astype(BF16)
    tm = xb.shape[0]
    lane_tile = lax.broadcasted_iota(jnp.int32, (1, LANES), 1)
    aqkv = _dot(xb, win_ref[:, C_AQKV:C_GATE])
    aqkv1_ref[...] = aqkv.astype(BF16)
    for sl in range(AQKV_SLABS):
        slab_ref[sl] = aqkv[:, sl * LANES:(sl + 1) * LANES]
    for d, ref in ((DILATIONS[1], aqkv4_ref), (DILATIONS[2], aqkv16_ref)):
        for r in range(d):
            for sl in range(AQKV_SLABS):
                ref[r, :, sl * LANES:(sl + 1) * LANES] = slab_ref[sl, pl.ds(r, tm // d, stride=d), :].astype(BF16)
    gate_ref[...] = _silu(_dot(xb, win_ref[:, C_GATE:C_CQ])).astype(BF16)
    lat = _dot(xb, win_ref[:, C_CQ:IN_COLS])
    cq = lat[:, 0:Q_LORA_RANK]
    ckv = lat[:, Q_LORA_RANK:Q_LORA_RANK + KV_LORA_RANK]
    rope = lat[:, Q_LORA_RANK + KV_LORA_RANK:]
    q2 = _dot(_rms(cq, gq_ref[...]).astype(BF16), wq_ref[...])
    qf_ref[...] = (q2[:, :B_HEADS * LANES] * cosq_ref[...] + q2[:, B_HEADS * LANES:] * sinq_ref[...]).astype(BF16)
    r2 = rope * csk_ref[...]
    kr = pltpu.roll(r2, B_NOPE_DIM, axis=1) + pltpu.roll(r2, B_NOPE_DIM - B_ROPE_DIM, axis=1)
    kr = jnp.where((lane_tile >= B_NOPE_DIM) & (lane_tile < B_NOPE_DIM + B_ROPE_DIM), kr, 0.0)
    kvn = _rms(ckv, gkv_ref[...]).astype(BF16)
    kn = _dot(kvn, wk_ref[...])
    for h in range(B_HEADS):
        kf_ref[:, h * LANES:(h + 1) * LANES] = (kn[:, h * LANES:(h + 1) * LANES] + kr).astype(BF16)
    vt_ref[...] = _dot_nt(wvt_ref[...], kvn).astype(BF16)


def _project(x, wts, tabs, tm):
    bsz, seq, _ = x.shape
    row = lambda w: pl.BlockSpec((None, tm, w), lambda i, b: (b, i, 0))
    res = lambda d: pl.BlockSpec((None, d, tm // d, 3 * A_WIDTH), lambda i, b: (b, 0, i, 0))
    full = lambda a: pl.BlockSpec(a.shape, lambda i, b: (0,) * a.ndim)
    tab = lambda w: pl.BlockSpec((tm, w), lambda i, b: (i, 0))
    out = lambda w: jax.ShapeDtypeStruct((bsz, seq, w), BF16)
    res_out = lambda d: jax.ShapeDtypeStruct((bsz, d, seq // d, 3 * A_WIDTH), BF16)
    return pl.pallas_call(
        _proj_kernel,
        grid=(seq // tm, bsz),
        in_specs=[row(D_MODEL), full(wts["win"]), full(wts["wq"]), full(wts["wk"]), full(wts["wvt"]),
                  full(wts["gq"]), full(wts["gkv"]),
                  tab(LANES), tab(B_HEADS * LANES), tab(B_HEADS * LANES)],
        out_specs=[row(3 * A_WIDTH), res(DILATIONS[1]), res(DILATIONS[2]), row(A_WIDTH + B_WIDTH),
                   row(B_HEADS * LANES), row(B_HEADS * LANES),
                   pl.BlockSpec((None, B_WIDTH, tm), lambda i, b: (b, 0, i))],
        out_shape=[out(3 * A_WIDTH), res_out(DILATIONS[1]), res_out(DILATIONS[2]), out(A_WIDTH + B_WIDTH),
                   out(B_HEADS * LANES), out(B_HEADS * LANES),
                   jax.ShapeDtypeStruct((bsz, B_WIDTH, seq), BF16)],
        scratch_shapes=[pltpu.VMEM((AQKV_SLABS, tm, LANES), F32)],
        compiler_params=pltpu.CompilerParams(dimension_semantics=("arbitrary", "arbitrary"),
                                             vmem_limit_bytes=VMEM_LIMIT),
        name="proj",
    )(x, wts["win"], wts["wq"], wts["wk"], wts["wvt"], wts["gq"], wts["gkv"],
      tabs["csk"], tabs["cosq"], tabs["sinq"])


def _dilated_kernel(q_ref, kc_ref, kp_ref, vc_ref, vp_ref, bias_ref, o_ref, lse_ref,
                    kk_ref, vt_ref, qq_ref, *, nsub, group):
    blk = pl.program_id(2)
    nres = q_ref.shape[0]
    lane = lax.broadcasted_iota(jnp.int32, (1, A_WIDTH), 1) % LANES
    lo = lane < A_HEAD_DIM
    for s in range(nres):
        for c in range(nsub):
            q = q_ref[s, c * BAND:(c + 1) * BAND, :]
            qq_ref[s, c, 0:BAND, :] = jnp.where(lo, q, jnp.zeros_like(q))
            qq_ref[s, c, BAND:, :] = jnp.where(lo, jnp.zeros_like(q), q)
        kk_ref[s, 0:BAND, :] = kp_ref[s]
        kk_ref[s, BAND:, :] = kc_ref[s]
        vt_ref[s, :, 0:BAND] = vp_ref[s].T
        vt_ref[s, :, BAND:] = vc_ref[s].T
    top = lax.broadcasted_iota(jnp.int32, (LANES, 1), 0) < A_HEAD_DIM
    first = (blk == 0).astype(jnp.int32)
    pair_cols = [slice(j * LANES, (j + 1) * LANES) for j in range(HEAD_PAIRS)]
    subs = [(s, c) for s in range(nres) for c in range(nsub)]
    for g0 in range(0, len(subs), group):
        blocks = [(s, c, j, slice(c * BAND, (c + 2) * BAND), pair_cols[j])
                  for s, c in subs[g0:g0 + group] for j in range(HEAD_PAIRS)]
        sts = [(_dot_nt(kk_ref[s, keys_rows, cols], qq_ref[s, c, :, cols])
                + bias_ref[first if c == 0 else 0, j]).astype(BF16)
               for s, c, j, keys_rows, cols in blocks]
        ms = [jnp.max(st, axis=0, keepdims=True) for st in sts]
        ps = [jnp.exp2(st - m) for st, m in zip(sts, ms)]
        ms = [m.astype(F32) for m in ms]
        ones = jnp.ones((ONES_ROWS, 2 * BAND), BF16)
        ot2s = [_dot(jnp.concatenate([vt_ref[s, cols, keys_rows], ones], axis=0), p)
                for (s, _, _, keys_rows, cols), p in zip(blocks, ps)]
        lses = {}
        for (s, c, j, _, cols), ot2, m in zip(blocks, ot2s, ms):
            l = ot2[LANES:LANES + 1, :]
            rl = 1.0 / l
            ot = jnp.where(top, ot2[0:LANES, 0:BAND] * rl[:, 0:BAND], ot2[0:LANES, BAND:] * rl[:, BAND:])
            o_ref[s, c * BAND:(c + 1) * BAND, cols] = ot.T.astype(BF16)
            lse = m + jnp.log2(l)
            lses.setdefault((s, c), []).extend([lse[:, 0:BAND], lse[:, BAND:]])
        for (s, c), rows in lses.items():
            lse_ref[s, :, c * BAND:(c + 1) * BAND] = jnp.concatenate(rows, axis=0)


def _dilated_bias(dilation):
    qi = np.arange(BAND)[None, :]
    kj = np.arange(2 * BAND)[:, None]
    rel = qi + BAND - kj
    valid = (rel >= 0) & (rel <= BAND)
    slopes = 2.0 ** (-8.0 * np.arange(1, A_HEADS + 1, dtype=np.float64) / A_HEADS)
    bias = -slopes[:, None, None] * (rel * dilation).astype(np.float64)[None] * LOG2E
    gen = np.where(valid[None], bias, NEG)
    fst = np.where((valid & (kj >= BAND))[None], bias, NEG)
    pair = lambda t: np.concatenate([t[0::2], t[1::2]], axis=-1)
    return jnp.asarray(np.stack([pair(gen), pair(fst)]), F32)


def _dilated(aqkv, tq):
    bsz, dilation, n, _ = aqkv.shape
    nsub = tq // BAND
    nres = min(dilation, DILATED_GROUP // nsub)
    cur = lambda which: pl.BlockSpec((None, nres, tq, A_WIDTH), lambda b, r, i: (b, r, i, which))
    prev = lambda which: pl.BlockSpec((None, nres, BAND, A_WIDTH),
                                      lambda b, r, i: (b, r, jnp.maximum(i * nsub - 1, 0), which))
    bias = _dilated_bias(dilation)
    return pl.pallas_call(
        functools.partial(_dilated_kernel, nsub=nsub, group=DILATED_GROUP),
        grid=(bsz, dilation // nres, n // tq),
        in_specs=[cur(0), cur(1), prev(1), cur(2), prev(2),
                  pl.BlockSpec(bias.shape, lambda b, r, i: (0, 0, 0, 0))],
        out_specs=[pl.BlockSpec((None, nres, tq, A_WIDTH), lambda b, r, i: (b, r, i, 0)),
                   pl.BlockSpec((None, nres, A_HEADS, tq), lambda b, r, i: (b, r, 0, i))],
        out_shape=[jax.ShapeDtypeStruct((bsz, dilation, n, A_WIDTH), BF16),
                   jax.ShapeDtypeStruct((bsz, dilation, A_HEADS, n), F32)],
        scratch_shapes=[pltpu.VMEM((nres, tq + BAND, A_WIDTH), BF16), pltpu.VMEM((nres, A_WIDTH, tq + BAND), BF16),
                        pltpu.VMEM((nres, nsub, 2 * BAND, A_WIDTH), BF16)],
        compiler_params=pltpu.CompilerParams(dimension_semantics=("arbitrary",) * 3,
                                             vmem_limit_bytes=VMEM_LIMIT),
        name=f"dilated_d{dilation}",
    )(aqkv, aqkv, aqkv, aqkv, aqkv, bias)


def _mla_kernel(qi_ref, kj_ref, q_ref, k_ref, vt_ref, g_ref, o_ref, m_ref, acc_ref, *, npair):
    t = pl.program_id(2)
    qi, kj = qi_ref[t], kj_ref[t]
    heads = range(2 * npair)
    acc_rows = lambda h: slice(h * ACC_ROWS, (h + 1) * ACC_ROWS)

    @pl.when(kj == 0)
    def _():
        m_ref[...] = jnp.full(m_ref.shape, NEG, F32)
        acc_ref[...] = jnp.zeros(acc_ref.shape, F32)

    half = q_ref.shape[0] // 2

    def scores(h, masked):
        cols = slice(h * LANES, (h + 1) * LANES)
        if not masked:
            return (_dot_nt(k_ref[:, cols], q_ref[:, cols]).astype(BF16),)
        tri = (lax.broadcasted_iota(jnp.int32, (half, half), 0) <= lax.broadcasted_iota(jnp.int32, (half, half), 1))
        top = _dot_nt(k_ref[0:half, cols], q_ref[:, cols])
        top = jnp.concatenate([jnp.where(tri, top[:, 0:half], NEG), top[:, half:]], axis=1)
        bot = jnp.where(tri, _dot_nt(k_ref[half:, cols], q_ref[half:, cols]), NEG)
        return top.astype(BF16), bot.astype(BF16)

    def accumulate(h, sts):
        v_ext = lambda keys: jnp.concatenate([vt_ref[h * B_V_DIM:(h + 1) * B_V_DIM, keys],
                                              jnp.ones((ONES_ROWS, keys.stop - keys.start), BF16)], axis=0)
        m_old = m_ref[h]
        colmax = lambda t: jnp.max(t, axis=0, keepdims=True).astype(F32)
        if len(sts) == 1:
            (st,) = sts
            m_new = jnp.maximum(m_old, colmax(st))
            pv = _dot(v_ext(slice(0, 2 * half)), jnp.exp2(st - m_new.astype(BF16)))
        else:
            top, bot = sts
            m_top = colmax(top)
            m_cur = jnp.concatenate([m_top[:, 0:half], jnp.maximum(m_top[:, half:], colmax(bot))], axis=1)
            m_new = jnp.maximum(m_old, m_cur)
            m_b = m_new.astype(BF16)
            pv = _dot(v_ext(slice(0, half)), jnp.exp2(top - m_b))
            pv_bot = _dot(v_ext(slice(half, 2 * half)), jnp.exp2(bot - m_b[:, half:]))
            pv = jnp.concatenate([pv[:, 0:half], pv[:, half:] + pv_bot], axis=1)
        alpha = jnp.exp2(m_old - m_new)
        m_ref[h] = m_new
        acc_ref[acc_rows(h), :] = alpha * acc_ref[acc_rows(h), :] + pv

    def step(masked):
        pending = {}
        for i in range(len(heads) + MLA_LOOKAHEAD):
            if i < len(heads):
                pending[i] = scores(heads[i], masked)
            if i >= MLA_LOOKAHEAD:
                accumulate(heads[i - MLA_LOOKAHEAD], pending.pop(i - MLA_LOOKAHEAD))

    @pl.when(kj < qi)
    def _():
        step(False)

    @pl.when(kj == qi)
    def _():
        step(True)
        ot = jnp.concatenate([acc_ref[h * ACC_ROWS:h * ACC_ROWS + B_V_DIM, :]
                              * (1.0 / acc_ref[h * ACC_ROWS + B_V_DIM:h * ACC_ROWS + B_V_DIM + 1, :])
                              for h in heads], axis=0)
        o_ref[...] = (ot.T * g_ref[...].astype(F32)).astype(BF16)


def _mla(qf, kf, vt, gates, tq, npair):
    bsz, seq, _ = qf.shape
    nq = seq // tq
    ngrp = HEAD_PAIRS // npair
    qi = np.concatenate([np.full(i + 1, i) for i in range(nq)]).astype(np.int32)
    kj = np.concatenate([np.arange(i + 1) for i in range(nq)]).astype(np.int32)
    wide, narrow = 2 * npair * LANES, npair * LANES
    grid_spec = pltpu.PrefetchScalarGridSpec(
        num_scalar_prefetch=2,
        grid=(bsz, ngrp, len(qi)),
        in_specs=[pl.BlockSpec((None, tq, wide), lambda b, p, t, qi, kj: (b, qi[t], p)),
                  pl.BlockSpec((None, tq, wide), lambda b, p, t, qi, kj: (b, kj[t], p)),
                  pl.BlockSpec((None, narrow, tq), lambda b, p, t, qi, kj: (b, p, kj[t])),
                  pl.BlockSpec((None, tq, narrow), lambda b, p, t, qi, kj: (b, qi[t], ngrp + p))],
        out_specs=pl.BlockSpec((None, tq, narrow), lambda b, p, t, qi, kj: (b, qi[t], p)),
        scratch_shapes=[pltpu.VMEM((2 * npair, 1, tq), F32), pltpu.VMEM((2 * npair * ACC_ROWS, tq), F32)],
    )
    return pl.pallas_call(
        functools.partial(_mla_kernel, npair=npair),
        grid_spec=grid_spec,
        out_shape=jax.ShapeDtypeStruct((bsz, seq, B_WIDTH), BF16),
        compiler_params=pltpu.CompilerParams(dimension_semantics=("arbitrary",) * 3,
                                             vmem_limit_bytes=VMEM_LIMIT),
        name="mla",
    )(jnp.asarray(qi), jnp.asarray(kj), qf, kf, vt, gates)


def _out_kernel(o1_ref, o4_ref, o16_ref, l1_ref, l4_ref, l16_ref, gate_ref, b_ref, x_ref,
                wo_ref, expand_ref, lng_ref, lnb_ref, out_ref, slab4_ref, slab16_ref):
    tm = out_ref.shape[0]
    nslab = A_WIDTH // LANES

    def unperm(o_ref, slab_ref):
        d = o_ref.shape[0]
        for r in range(d):
            o = o_ref[r].astype(F32)
            for sl in range(nslab):
                slab_ref[sl, pl.ds(r, tm // d, stride=d), :] = o[:, sl * LANES:(sl + 1) * LANES]

    unperm(o4_ref, slab4_ref)
    unperm(o16_ref, slab16_ref)
    l1, l4, l16 = l1_ref[...], l4_ref[...], l16_ref[...]
    m = jnp.maximum(jnp.maximum(l1, l4), l16)
    e1, e4, e16 = jnp.exp2(l1 - m), jnp.exp2(l4 - m), jnp.exp2(l16 - m)
    rden = 1.0 / (e1 + e4 + e16)
    pad = jnp.zeros((2 * SUBLANES - A_HEADS, tm), F32)
    wts = [jnp.concatenate([e * rden, pad], axis=0).astype(BF16) for e in (e1, e4, e16)]

    def merge(rows):
        expand = lambda wt: lax.dot_general(wt[:, rows], expand_ref[...], TN_DIMS, preferred_element_type=F32)
        slab = lambda ref: jnp.concatenate([ref[sl, rows, :] for sl in range(nslab)], axis=1)
        a = expand(wts[0]) * o1_ref[rows, :].astype(F32)
        a = a + expand(wts[1]) * slab(slab4_ref) + expand(wts[2]) * slab(slab16_ref)
        return (a * gate_ref[rows, 0:A_WIDTH].astype(F32)).astype(BF16)

    def project(rows, a):
        return _dot(a, wo_ref[0:A_WIDTH, :]) + _dot(b_ref[rows, :], wo_ref[A_WIDTH:, :])

    def norm(rows, y):
        z = DEEPNORM_ALPHA * x_ref[rows, :] + y
        mu = jnp.mean(z, axis=-1, keepdims=True)
        zc = z - mu
        var = jnp.mean(zc * zc, axis=-1, keepdims=True)
        out_ref[rows, :] = zc * lax.rsqrt(var + LN_EPS) * lng_ref[...] + lnb_ref[...]

    chunks = [slice(k * OUT_CHUNK, (k + 1) * OUT_CHUNK) for k in range(tm // OUT_CHUNK)]
    merged, projected = {}, {}
    for k in range(len(chunks) + 2):
        if k < len(chunks):
            merged[k] = merge(chunks[k])
        if 1 <= k <= len(chunks):
            projected[k - 1] = project(chunks[k - 1], merged.pop(k - 1))
        if k >= 2:
            norm(chunks[k - 2], projected.pop(k - 2))


def _finish(os_, lses, gates, b_out, x, wo, expand, ln_g, ln_b, tm):
    bsz, seq, _ = x.shape
    row = lambda w: pl.BlockSpec((None, tm, w), lambda b, i: (b, i, 0))
    res = lambda d: pl.BlockSpec((None, d, tm // d, A_WIDTH), lambda b, i: (b, 0, i, 0))
    lse = pl.BlockSpec((None, A_HEADS, tm), lambda b, i: (b, 0, i))
    full = lambda a: pl.BlockSpec(a.shape, lambda b, i: (0,) * a.ndim)
    return pl.pallas_call(
        _out_kernel,
        grid=(bsz, seq // tm),
        in_specs=[row(A_WIDTH), res(DILATIONS[1]), res(DILATIONS[2]),
                  lse, lse, lse, row(A_WIDTH), row(B_WIDTH), row(D_MODEL),
                  full(wo), full(expand), full(ln_g), full(ln_b)],
        out_specs=row(D_MODEL),
        out_shape=jax.ShapeDtypeStruct((bsz, seq, D_MODEL), F32),
        scratch_shapes=[pltpu.VMEM((A_WIDTH // LANES, tm, LANES), F32)] * 2,
        compiler_params=pltpu.CompilerParams(dimension_semantics=("arbitrary", "arbitrary"),
                                             vmem_limit_bytes=VMEM_LIMIT),
        name="merge_out_ln",
    )(*os_, *lses, gates, b_out, x, wo, expand, ln_g, ln_b)


def _rot_cols(w):
    half = w.shape[-1] // 2
    return jnp.concatenate([-w[..., half:], w[..., :half]], axis=-1)


def _layout_weights(w_in, q_norm_g, w_uq, kv_norm_g, w_ukv, w_o):
    aw = A_WIDTH
    a_q, a_k, a_v, a_gate = (w_in[:, i * aw:(i + 1) * aw] for i in range(4))
    o = 4 * aw
    c_q, o = w_in[:, o:o + Q_LORA_RANK], o + Q_LORA_RANK
    c_kv, o = w_in[:, o:o + KV_LORA_RANK], o + KV_LORA_RANK
    k_rope, o = w_in[:, o:o + B_ROPE_DIM], o + B_ROPE_DIM
    b_gate = w_in[:, o:o + B_WIDTH]
    pad = jnp.zeros((D_MODEL, LANES - 2 * B_ROPE_DIM), w_in.dtype)
    win = jnp.concatenate([a_q * (A_HEAD_DIM ** -0.5 * LOG2E), a_k, a_v, a_gate, b_gate, c_q, c_kv,
                           k_rope, _rot_cols(k_rope), pad], axis=1).astype(BF16)
    uq = w_uq.reshape(Q_LORA_RANK, B_HEADS, B_NOPE_DIM + B_ROPE_DIM)
    nope, rope = uq[..., :B_NOPE_DIM], uq[..., B_NOPE_DIM:]
    z = lambda w: jnp.zeros((Q_LORA_RANK, B_HEADS, w), w_uq.dtype)
    tail = LANES - B_NOPE_DIM - B_ROPE_DIM
    w1 = jnp.concatenate([nope, rope, z(tail)], axis=-1).reshape(Q_LORA_RANK, B_HEADS * LANES)
    w2 = jnp.concatenate([z(B_NOPE_DIM), _rot_cols(rope), z(tail)], axis=-1).reshape(Q_LORA_RANK, B_HEADS * LANES)
    wq = jnp.concatenate([w1, w2], axis=1).astype(BF16)
    ukv = w_ukv.reshape(KV_LORA_RANK, B_HEADS, B_NOPE_DIM + B_V_DIM)
    wk = jnp.concatenate([ukv[..., :B_NOPE_DIM], jnp.zeros((KV_LORA_RANK, B_HEADS, LANES - B_NOPE_DIM), w_ukv.dtype)],
                         axis=-1).reshape(KV_LORA_RANK, B_HEADS * LANES)
    wvt = ukv[..., B_NOPE_DIM:].reshape(KV_LORA_RANK, B_WIDTH).T
    expand =np.zeros((2 * SUBLANES, A_WIDTH), np.float32)
    for h in range(A_HEADS):
        expand[h, h * A_HEAD_DIM:(h + 1) * A_HEAD_DIM] = 1.0
    return {"win": win, "wq": wq, "wk": wk.astype(BF16), "wvt": wvt.astype(BF16),
            "gq": q_norm_g.reshape(1, -1).astype(F32), "gkv": kv_norm_g.reshape(1, -1).astype(F32),
            "wo": w_o.astype(BF16), "expand": jnp.asarray(expand, BF16)}


def _rope_tables(seq):
    inv_freq = ROPE_THETA ** (-jnp.arange(0, B_ROPE_DIM, 2, dtype=F32) / B_ROPE_DIM)
    ang = jnp.arange(seq, dtype=F32)[:, None] * inv_freq[None, :]
    ang = jnp.concatenate([ang, ang], axis=-1)
    cos, sin = jnp.cos(ang), jnp.sin(ang)
    scale = (B_NOPE_DIM + B_ROPE_DIM) ** -0.5 * LOG2E
    one = jnp.ones((seq, B_NOPE_DIM), F32)
    zn = jnp.zeros((seq, B_NOPE_DIM), F32)
    zt = jnp.zeros((seq, LANES - B_NOPE_DIM - B_ROPE_DIM), F32)
    cosq = jnp.tile(jnp.concatenate([one, cos, zt], axis=1) * scale, (1, B_HEADS))
    sinq = jnp.tile(jnp.concatenate([zn, sin, zt], axis=1) * scale, (1, B_HEADS))
    csk = jnp.concatenate([cos, sin, jnp.zeros((seq, LANES - 2 * B_ROPE_DIM), F32)], axis=1)
    return {"csk": csk, "cosq": cosq, "sinq": sinq}


def kernel(x, w_in, q_norm_g, w_uq, kv_norm_g, w_ukv, w_o, ln_g, ln_b):
    bsz, seq, _ = x.shape
    wts = _layout_weights(w_in, q_norm_g, w_uq, kv_norm_g, w_ukv, w_o)
    tabs = _rope_tables(seq)
    aqkv1, aqkv4, aqkv16, gates, qf, kf, vt = _project(x, wts, tabs, tm=512)
    os_, lses = [], []
    for aqkv in (aqkv1[:, None], aqkv4, aqkv16):
        d, n = aqkv.shape[1], aqkv.shape[2]
        o, lse_t = _dilated(aqkv, tq=min(DILATED_TQ, n))
        os_.append(o)
        lses.append(jnp.transpose(lse_t, (0, 2, 3, 1)).reshape(bsz, A_HEADS, seq))
    os_[0] = os_[0][:, 0]
    b_out = _mla(qf, kf, vt, gates, tq=512, npair=4)
    return _finish(os_, lses, gates, b_out, x, wts["wo"], wts["expand"],
                   ln_g.reshape(1, -1).astype(F32), ln_b.reshape(1, -1).astype(F32), tm=512)
```

```python
import functools

import jax
import jax.numpy as jnp
import numpy as np
from jax import lax
from jax.experimental import pallas as pl
from jax.experimental.pallas import tpu as pltpu

D_MODEL = 1024
A_HEADS = 8
A_HEAD_DIM = 64
A_WIDTH = A_HEADS * A_HEAD_DIM
DILATED_PATTERNS = ((128, 1), (512, 4), (2048, 16))
DILATIONS = tuple(d for _, d in DILATED_PATTERNS)
BAND = 128
B_HEADS = 8
B_NOPE_DIM = 64
B_ROPE_DIM = 32
B_V_DIM = 64
B_WIDTH = B_HEADS * B_V_DIM
Q_LORA_RANK = 256
KV_LORA_RANK = 128
ROPE_THETA = 10000.0
LN_EPS = 1e-5
RMS_EPS = 1e-6
DEPTH = 1
DEEPNORM_ALPHA = (2 * DEPTH) ** 0.25

LANES = 128
SUBLANES = 8
HEAD_PAIRS = A_HEADS // 2
NEG = -1e30
VMEM_LIMIT = 56 * 1024 * 1024

BF16 = jnp.bfloat16
F32 = jnp.float32

C_AQKV = 0
C_AGATE = 3 * A_WIDTH
C_CQ = 4 * A_WIDTH
C_TAIL = C_CQ + Q_LORA_RANK + KV_LORA_RANK
IN_COLS = 3072
AQKV_SLABS = 3 * A_WIDTH // LANES

TN_DIMS = (((0,), (0,)), ((), ()))
LOG2E = float(np.log2(np.e))
ONES_ROWS = 16
ACC_ROWS = B_V_DIM + ONES_ROWS
MLA_LOOKAHEAD = 2
OUT_CHUNK = 128
DILATED_GROUP = 8
DILATED_TQ = DILATED_GROUP * BAND


def _dot(a, b):
    return jnp.dot(a, b, preferred_element_type=F32)


def _dot_nt(a, b):
    return lax.dot_general(a, b, (((1,), (1,)), ((), ())), preferred_element_type=F32)


def _rms(t, g):
    return t * lax.rsqrt(jnp.mean(t * t, axis=-1, keepdims=True) + RMS_EPS) * g


def _silu(t):
    return t / (1.0 + jnp.exp(-t))


def _proj_kernel(x_ref, win_ref, wq_ref, wk_ref, wvt_ref, gq_ref, gkv_ref, csk_ref, cosq_ref, sinq_ref,
                 aqkv1_ref, aqkv4_ref, aqkv16_ref, gate_ref, qf_ref, kf_ref, vt_ref, slab_ref):
    xb = x_ref[...].astype(BF16)
    tm = xb.shape[0]
    lane_tile = lax.broadcasted_iota(jnp.int32, (1, LANES), 1)
    aqkv = _dot(xb, win_ref[:, C_AQKV:C_AGATE])
    aqkv = jnp.concatenate([aqkv[:, :A_WIDTH] * (A_HEAD_DIM ** -0.5 * LOG2E), aqkv[:, A_WIDTH:]], axis=1)
    aqkv1_ref[...] = aqkv.astype(BF16)
    for sl in range(AQKV_SLABS):
        slab_ref[sl] = aqkv[:, sl * LANES:(sl + 1) * LANES]
    for d, ref in ((DILATIONS[1], aqkv4_ref), (DILATIONS[2], aqkv16_ref)):
        for r in range(d):
            for sl in range(AQKV_SLABS):
                ref[r, :, sl * LANES:(sl + 1) * LANES] = slab_ref[sl, pl.ds(r, tm // d, stride=d), :].astype(BF16)
    rest = _dot(xb, win_ref[:, C_AGATE:IN_COLS])
    off = lambda c: c - C_AGATE
    gate_ref[:, 0:A_WIDTH] = _silu(rest[:, 0:off(C_CQ)]).astype(BF16)
    gate_ref[:, A_WIDTH:] = _silu(rest[:, off(C_TAIL):off(C_TAIL) + B_WIDTH]).astype(BF16)
    cq = rest[:, off(C_CQ):off(C_CQ) + Q_LORA_RANK]
    ckv = rest[:, off(C_CQ) + Q_LORA_RANK:off(C_TAIL)]
    rk = jnp.where(lane_tile < B_ROPE_DIM, rest[:, off(C_TAIL) + B_WIDTH:], 0.0)
    hr = B_ROPE_DIM // 2
    rope = (rk + jnp.where((lane_tile >= B_ROPE_DIM) & (lane_tile < B_ROPE_DIM + hr), -pltpu.roll(rk, hr, axis=1), 0.0)
            + jnp.where((lane_tile >= B_ROPE_DIM + hr) & (lane_tile < 2 * B_ROPE_DIM),
                        pltpu.roll(rk, B_ROPE_DIM + hr, axis=1), 0.0))
    q2 = _dot(_rms(cq, gq_ref[...]).astype(BF16), wq_ref[...])
    cosq, sinq = jnp.tile(cosq_ref[...], (1, B_HEADS)), jnp.tile(sinq_ref[...], (1, B_HEADS))
    qf_ref[...] = (q2[:, :B_HEADS * LANES] * cosq + q2[:, B_HEADS * LANES:] * sinq).astype(BF16)
    r2 = rope * csk_ref[...]
    kr = pltpu.roll(r2, B_NOPE_DIM, axis=1) + pltpu.roll(r2, B_NOPE_DIM - B_ROPE_DIM, axis=1)
    kr = jnp.where((lane_tile >= B_NOPE_DIM) & (lane_tile < B_NOPE_DIM + B_ROPE_DIM), kr, 0.0)
    kvn = _rms(ckv, gkv_ref[...]).astype(BF16)
    kn = _dot(kvn, wk_ref[...])
    for h in range(B_HEADS):
        kf_ref[:, h * LANES:(h + 1) * LANES] = (kn[:, h * LANES:(h + 1) * LANES] + kr).astype(BF16)
    vt_ref[...] = _dot_nt(wvt_ref[...], kvn).astype(BF16)


def _project(x, wts, tabs, tm):
    bsz, seq, _ = x.shape
    row = lambda w: pl.BlockSpec((None, tm, w), lambda i, b: (b, i, 0))
    res = lambda d: pl.BlockSpec((None, d, tm // d, 3 * A_WIDTH), lambda i, b: (b, 0, i, 0))
    full = lambda a: pl.BlockSpec(a.shape, lambda i, b: (0,) * a.ndim)
    tab = lambda w: pl.BlockSpec((tm, w), lambda i, b: (i, 0))
    out = lambda w: jax.ShapeDtypeStruct((bsz, seq, w), BF16)
    res_out = lambda d: jax.ShapeDtypeStruct((bsz, d, seq // d, 3 * A_WIDTH), BF16)
    return pl.pallas_call(
        _proj_kernel,
        grid=(seq // tm, bsz),
        in_specs=[row(D_MODEL), full(wts["win"]), full(wts["wq"]), full(wts["wk"]), full(wts["wvt"]),
                  full(wts["gq"]), full(wts["gkv"]),
                  tab(LANES), tab(LANES), tab(LANES)],
        out_specs=[row(3 * A_WIDTH), res(DILATIONS[1]), res(DILATIONS[2]), row(A_WIDTH + B_WIDTH),
                   row(B_HEADS * LANES), row(B_HEADS * LANES),
                   pl.BlockSpec((None, B_WIDTH, tm), lambda i, b: (b, 0, i))],
        out_shape=[out(3 * A_WIDTH), res_out(DILATIONS[1]), res_out(DILATIONS[2]), out(A_WIDTH + B_WIDTH),
                   out(B_HEADS * LANES), out(B_HEADS * LANES),
                   jax.ShapeDtypeStruct((bsz, B_WIDTH, seq), BF16)],
        scratch_shapes=[pltpu.VMEM((AQKV_SLABS, tm, LANES), F32)],
        compiler_params=pltpu.CompilerParams(dimension_semantics=("arbitrary", "arbitrary"),
                                             vmem_limit_bytes=VMEM_LIMIT),
        name="proj",
    )(x, wts["win"], wts["wq"], wts["wk"], wts["wvt"], wts["gq"], wts["gkv"],
      tabs["csk"], tabs["cosq"], tabs["sinq"])


def _dilated_kernel(q_ref, kc_ref, kp_ref, vc_ref, vp_ref, bias_ref, o_ref, lse_ref,
                    kk_ref, vt_ref, qq_ref, *, nsub, group):
    blk = pl.program_id(2)
    nres = q_ref.shape[0]
    lane = lax.broadcasted_iota(jnp.int32, (1, A_WIDTH), 1) % LANES
    lo = lane < A_HEAD_DIM
    for s in range(nres):
        for c in range(nsub):
            q = q_ref[s, c * BAND:(c + 1) * BAND, :]
            qq_ref[s, c, 0:BAND, :] = jnp.where(lo, q, jnp.zeros_like(q))
            qq_ref[s, c, BAND:, :] = jnp.where(lo, jnp.zeros_like(q), q)
        kk_ref[s, 0:BAND, :] = kp_ref[s]
        kk_ref[s, BAND:, :] = kc_ref[s]
        vt_ref[s, :, 0:BAND] = vp_ref[s].T
        vt_ref[s, :, BAND:] = vc_ref[s].T
    top = lax.broadcasted_iota(jnp.int32, (LANES, 1), 0) < A_HEAD_DIM
    first = (blk == 0).astype(jnp.int32)
    pair_cols = [slice(j * LANES, (j + 1) * LANES) for j in range(HEAD_PAIRS)]
    subs = [(s, c) for s in range(nres) for c in range(nsub)]
    for g0 in range(0, len(subs), group):
        blocks = [(s, c, j, slice(c * BAND, (c + 2) * BAND), pair_cols[j])
                  for s, c in subs[g0:g0 + group] for j in range(HEAD_PAIRS)]
        sts = [(_dot_nt(kk_ref[s, keys_rows, cols], qq_ref[s, c, :, cols])
                + bias_ref[first if c == 0 else 0, j]).astype(BF16)
               for s, c, j, keys_rows, cols in blocks]
        ms = [jnp.max(st, axis=0, keepdims=True) for st in sts]
        ps = [jnp.exp2(st - m) for st, m in zip(sts, ms)]
        ms = [m.astype(F32) for m in ms]
        ones = jnp.ones((ONES_ROWS, 2 * BAND), BF16)
        ot2s = [_dot(jnp.concatenate([vt_ref[s, cols, keys_rows], ones], axis=0), p)
                for (s, _, _, keys_rows, cols), p in zip(blocks, ps)]
        lses = {}
        for (s, c, j, _, cols), ot2, m in zip(blocks, ot2s, ms):
            l = ot2[LANES:LANES + 1, :]
            rl = 1.0 / l
            ot = jnp.where(top, ot2[0:LANES, 0:BAND] * rl[:, 0:BAND], ot2[0:LANES, BAND:] * rl[:, BAND:])
            o_ref[s, c * BAND:(c + 1) * BAND, cols] = ot.T.astype(BF16)
            lse = m + jnp.log2(l)
            lses.setdefault((s, c), []).extend([lse[:, 0:BAND], lse[:, BAND:]])
        for (s, c), rows in lses.items():
            lse_ref[s, :, c * BAND:(c + 1) * BAND] = jnp.concatenate(rows, axis=0)


def _dilated_bias(dilation):
    qi = np.arange(BAND)[None, :]
    kj = np.arange(2 * BAND)[:, None]
    rel = qi + BAND - kj
    valid = (rel >= 0) & (rel <= BAND)
    slopes = 2.0 ** (-8.0 * np.arange(1, A_HEADS + 1, dtype=np.float64) / A_HEADS)
    bias = -slopes[:, None, None] * (rel * dilation).astype(np.float64)[None] * LOG2E
    gen = np.where(valid[None], bias, NEG)
    fst = np.where((valid & (kj >= BAND))[None], bias, NEG)
    pair = lambda t: np.concatenate([t[0::2], t[1::2]], axis=-1)
    return jnp.asarray(np.stack([pair(gen), pair(fst)]), F32)


def _dilated(aqkv, tq):
    bsz, dilation, n, _ = aqkv.shape
    nsub = tq // BAND
    nres = min(dilation, DILATED_GROUP // nsub)
    cur = lambda which: pl.BlockSpec((None, nres, tq, A_WIDTH), lambda b, r, i: (b, r, i, which))
    prev = lambda which: pl.BlockSpec((None, nres, BAND, A_WIDTH),
                                      lambda b, r, i: (b, r, jnp.maximum(i * nsub - 1, 0), which))
    bias = _dilated_bias(dilation)
    return pl.pallas_call(
        functools.partial(_dilated_kernel, nsub=nsub, group=DILATED_GROUP),
        grid=(bsz, dilation // nres, n // tq),
        in_specs=[cur(0), cur(1), prev(1), cur(2), prev(2),
                  pl.BlockSpec(bias.shape, lambda b, r, i: (0, 0, 0, 0))],
        out_specs=[pl.BlockSpec((None, nres, tq, A_WIDTH), lambda b, r, i: (b, r, i, 0)),
                   pl.BlockSpec((None, nres, A_HEADS, tq), lambda b, r, i: (b, r, 0, i))],
        out_shape=[jax.ShapeDtypeStruct((bsz, dilation, n, A_WIDTH), BF16),
                   jax.ShapeDtypeStruct((bsz, dilation, A_HEADS, n), F32)],
        scratch_shapes=[pltpu.VMEM((nres, tq + BAND, A_WIDTH), BF16), pltpu.VMEM((nres, A_WIDTH, tq + BAND), BF16),
                        pltpu.VMEM((nres, nsub, 2 * BAND, A_WIDTH), BF16)],
        compiler_params=pltpu.CompilerParams(dimension_semantics=("arbitrary",) * 3,
                                             vmem_limit_bytes=VMEM_LIMIT),
        name=f"dilated_d{dilation}",
    )(aqkv, aqkv, aqkv, aqkv, aqkv, bias)


def _mla_kernel(qi_ref, kj_ref, q_ref, k_ref, vt_ref, g_ref, o_ref, m_ref, acc_ref, *, npair):
    t = pl.program_id(2)
    qi, kj = qi_ref[t], kj_ref[t]
    heads = range(2 * npair)
    acc_rows = lambda h: slice(h * ACC_ROWS, (h + 1) * ACC_ROWS)

    @pl.when(kj == 0)
    def _():
        m_ref[...] = jnp.full(m_ref.shape, NEG, F32)
        acc_ref[...] = jnp.zeros(acc_ref.shape, F32)

    half = q_ref.shape[0] // 2

    def scores(h, masked):
        cols = slice(h * LANES, (h + 1) * LANES)
        if not masked:
            return (_dot_nt(k_ref[:, cols], q_ref[:, cols]).astype(BF16),)
        tri = (lax.broadcasted_iota(jnp.int32, (half, half), 0) <= lax.broadcasted_iota(jnp.int32, (half, half), 1))
        top = _dot_nt(k_ref[0:half, cols], q_ref[:, cols])
        top = jnp.concatenate([jnp.where(tri, top[:, 0:half], NEG), top[:, half:]], axis=1)
        bot = jnp.where(tri, _dot_nt(k_ref[half:, cols], q_ref[half:, cols]), NEG)
        return top.astype(BF16), bot.astype(BF16)

    def accumulate(h, sts):
        v_ext = lambda keys: jnp.concatenate([vt_ref[h * B_V_DIM:(h + 1) * B_V_DIM, keys],
                                              jnp.ones((ONES_ROWS, keys.stop - keys.start), BF16)], axis=0)
        m_old = m_ref[h]
        colmax = lambda t: jnp.max(t, axis=0, keepdims=True).astype(F32)
        if len(sts) == 1:
            (st,) = sts
            m_new = jnp.maximum(m_old, colmax(st))
            pv = _dot(v_ext(slice(0, 2 * half)), jnp.exp2(st - m_new.astype(BF16)))
        else:
            top, bot = sts
            m_top = colmax(top)
            m_cur = jnp.concatenate([m_top[:, 0:half], jnp.maximum(m_top[:, half:], colmax(bot))], axis=1)
            m_new = jnp.maximum(m_old, m_cur)
            m_b = m_new.astype(BF16)
            pv = _dot(v_ext(slice(0, half)), jnp.exp2(top - m_b))
            pv_bot = _dot(v_ext(slice(half, 2 * half)), jnp.exp2(bot - m_b[:, half:]))
            pv = jnp.concatenate([pv[:, 0:half], pv[:, half:] + pv_bot], axis=1)
        alpha = jnp.exp2(m_old - m_new)
        m_ref[h] = m_new
        acc_ref[acc_rows(h), :] = alpha * acc_ref[acc_rows(h), :] + pv

    def step(masked):
        pending = {}
        for i in range(len(heads) + MLA_LOOKAHEAD):
            if i < len(heads):
                pending[i] = scores(heads[i], masked)
            if i >= MLA_LOOKAHEAD:
                accumulate(heads[i - MLA_LOOKAHEAD], pending.pop(i - MLA_LOOKAHEAD))

    @pl.when(kj < qi)
    def _():
        step(False)

    @pl.when(kj == qi)
    def _():
        step(True)
        ot = jnp.concatenate([acc_ref[h * ACC_ROWS:h * ACC_ROWS + B_V_DIM, :]
                              * (1.0 / acc_ref[h * ACC_ROWS + B_V_DIM:h * ACC_ROWS + B_V_DIM + 1, :])
                              for h in heads], axis=0)
        o_ref[...] = (ot.T * g_ref[...].astype(F32)).astype(BF16)


def _mla(qf, kf, vt, gates, tq, npair):
    bsz, seq, _ = qf.shape
    nq = seq // tq
    ngrp = HEAD_PAIRS // npair
    qi = np.concatenate([np.full(i + 1, i) for i in range(nq)]).astype(np.int32)
    kj = np.concatenate([np.arange(i + 1) for i in range(nq)]).astype(np.int32)
    wide, narrow = 2 * npair * LANES, npair * LANES
    grid_spec = pltpu.PrefetchScalarGridSpec(
        num_scalar_prefetch=2,
        grid=(bsz, ngrp, len(qi)),
        in_specs=[pl.BlockSpec((None, tq, wide), lambda b, p, t, qi, kj: (b, qi[t], p)),
                  pl.BlockSpec((None, tq, wide), lambda b, p, t, qi, kj: (b, kj[t], p)),
                  pl.BlockSpec((None, narrow, tq), lambda b, p, t, qi, kj: (b, p, kj[t])),
                  pl.BlockSpec((None, tq, narrow), lambda b, p, t, qi, kj: (b, qi[t], ngrp + p))],
        out_specs=pl.BlockSpec((None, tq, narrow), lambda b, p, t, qi, kj: (b, qi[t], p)),
        scratch_shapes=[pltpu.VMEM((2 * npair, 1, tq), F32), pltpu.VMEM((2 * npair * ACC_ROWS, tq), F32)],
    )
    return pl.pallas_call(
        functools.partial(_mla_kernel, npair=npair),
        grid_spec=grid_spec,
        out_shape=jax.ShapeDtypeStruct((bsz, seq, B_WIDTH), BF16),
        compiler_params=pltpu.CompilerParams(dimension_semantics=("arbitrary",) * 3,
                                             vmem_limit_bytes=VMEM_LIMIT),
        name="mla",
    )(jnp.asarray(qi), jnp.asarray(kj), qf, kf, vt, gates)


def _out_kernel(o1_ref, o4_ref, o16_ref, l1_ref, l4_ref, l16_ref, gate_ref, b_ref, x_ref,
                wo_ref, expand_ref, lng_ref, lnb_ref, out_ref, slab4_ref, slab16_ref):
    tm = out_ref.shape[0]
    nslab = A_WIDTH // LANES

    def unperm(o_ref, slab_ref):
        d = o_ref.shape[0]
        for r in range(d):
            o = o_ref[r].astype(F32)
            for sl in range(nslab):
                slab_ref[sl, pl.ds(r, tm // d, stride=d), :] = o[:, sl * LANES:(sl + 1) * LANES]

    unperm(o4_ref, slab4_ref)
    unperm(o16_ref, slab16_ref)
    l1, l4, l16 = l1_ref[...], l4_ref[...], l16_ref[...]
    m = jnp.maximum(jnp.maximum(l1, l4), l16)
    e1, e4, e16 = jnp.exp2(l1 - m), jnp.exp2(l4 - m), jnp.exp2(l16 - m)
    rden = 1.0 / (e1 + e4 + e16)
    pad = jnp.zeros((2 * SUBLANES - A_HEADS, tm), F32)
    wts = [jnp.concatenate([e * rden, pad], axis=0).astype(BF16) for e in (e1, e4, e16)]

    def merge(rows):
        expand = lambda wt: lax.dot_general(wt[:, rows], expand_ref[...], TN_DIMS, preferred_element_type=F32)
        slab = lambda ref: jnp.concatenate([ref[sl, rows, :] for sl in range(nslab)], axis=1)
        a = expand(wts[0]) * o1_ref[rows, :].astype(F32)
        a = a + expand(wts[1]) * slab(slab4_ref) + expand(wts[2]) * slab(slab16_ref)
        return (a * gate_ref[rows, 0:A_WIDTH].astype(F32)).astype(BF16)

    def project(rows, a):
        return _dot(a, wo_ref[0:A_WIDTH, :]) + _dot(b_ref[rows, :], wo_ref[A_WIDTH:, :])

    def norm(rows, y):
        z = DEEPNORM_ALPHA * x_ref[rows, :] + y
        mu = jnp.mean(z, axis=-1, keepdims=True)
        zc = z - mu
        var = jnp.mean(zc * zc, axis=-1, keepdims=True)
        out_ref[rows, :] = zc * lax.rsqrt(var + LN_EPS) * lng_ref[...] + lnb_ref[...]

    chunks = [slice(k * OUT_CHUNK, (k + 1) * OUT_CHUNK) for k in range(tm // OUT_CHUNK)]
    merged, projected = {}, {}
    for k in range(len(chunks) + 2):
        if k < len(chunks):
            merged[k] = merge(chunks[k])
        if 1 <= k <= len(chunks):
            projected[k - 1] = project(chunks[k - 1], merged.pop(k - 1))
        if k >= 2:
            norm(chunks[k - 2], projected.pop(k - 2))


def _finish(os_, lses, gates, b_out, x, wo, expand, ln_g, ln_b, tm):
    bsz, seq, _ = x.shape
    row = lambda w: pl.BlockSpec((None, tm, w), lambda b, i: (b, i, 0))
    res = lambda d: pl.BlockSpec((None, d, tm // d, A_WIDTH), lambda b, i: (b, 0, i, 0))
    lse = pl.BlockSpec((None, A_HEADS, tm), lambda b, i: (b, 0, i))
    full = lambda a: pl.BlockSpec(a.shape, lambda b, i: (0,) * a.ndim)
    return pl.pallas_call(
        _out_kernel,
        grid=(bsz, seq // tm),
        in_specs=[row(A_WIDTH), res(DILATIONS[1]), res(DILATIONS[2]),
                  lse, lse, lse, row(A_WIDTH), row(B_WIDTH), row(D_MODEL),
                  full(wo), full(expand), full(ln_g), full(ln_b)],
        out_specs=row(D_MODEL),
        out_shape=jax.ShapeDtypeStruct((bsz, seq, D_MODEL), F32),
        scratch_shapes=[pltpu.VMEM((A_WIDTH // LANES, tm, LANES), F32)] * 2,
        compiler_params=pltpu.CompilerParams(dimension_semantics=("arbitrary", "arbitrary"),
                                             vmem_limit_bytes=VMEM_LIMIT),
        name="merge_out_ln",
    )(*os_, *lses, gates, b_out, x, wo, expand, ln_g, ln_b)


def _rot_cols(w):
    half = w.shape[-1] // 2
    return jnp.concatenate([-w[..., half:], w[..., :half]], axis=-1)


def _layout_weights(w_in, q_norm_g, w_uq, kv_norm_g, w_ukv, w_o):
    head, k_rope, b_gate = w_in[:, :C_TAIL], w_in[:, C_TAIL:C_TAIL + B_ROPE_DIM], w_in[:, C_TAIL + B_ROPE_DIM:]
    win = jnp.pad(jnp.concatenate([head, b_gate, k_rope], axis=1).astype(BF16),
                  ((0, 0), (0, IN_COLS - w_in.shape[1])))
    uq =w_uq.reshape(Q_LORA_RANK, B_HEADS, B_NOPE_DIM + B_ROPE_DIM)
    nope, rope = uq[..., :B_NOPE_DIM], uq[..., B_NOPE_DIM:]
    z = lambda w: jnp.zeros((Q_LORA_RANK, B_HEADS, w), w_uq.dtype)
    tail = LANES - B_NOPE_DIM - B_ROPE_DIM
    w1 = jnp.concatenate([nope, rope, z(tail)], axis=-1).reshape(Q_LORA_RANK, B_HEADS * LANES)
    w2 = jnp.concatenate([z(B_NOPE_DIM), _rot_cols(rope), z(tail)], axis=-1).reshape(Q_LORA_RANK, B_HEADS * LANES)
    wq = jnp.concatenate([w1, w2], axis=1).astype(BF16)
    ukv = w_ukv.reshape(KV_LORA_RANK, B_HEADS, B_NOPE_DIM + B_V_DIM)
    wk = jnp.concatenate([ukv[..., :B_NOPE_DIM], jnp.zeros((KV_LORA_RANK, B_HEADS, LANES - B_NOPE_DIM), w_ukv.dtype)],
                         axis=-1).reshape(KV_LORA_RANK, B_HEADS * LANES)
    wvt = ukv[..., B_NOPE_DIM:].reshape(KV_LORA_RANK, B_WIDTH).T
    expand =np.zeros((2 * SUBLANES, A_WIDTH), np.float32)
    for h in range(A_HEADS):
        expand[h, h * A_HEAD_DIM:(h + 1) * A_HEAD_DIM] = 1.0
    return {"win": win, "wq": wq, "wk": wk.astype(BF16), "wvt": wvt.astype(BF16),
            "gq": q_norm_g.reshape(1, -1).astype(F32), "gkv": kv_norm_g.reshape(1, -1).astype(F32),
            "wo": w_o.astype(BF16), "expand": jnp.asarray(expand, BF16)}


def _rope_tables(seq):
    inv_freq = ROPE_THETA ** (-np.arange(0, B_ROPE_DIM, 2, dtype=np.float64) / B_ROPE_DIM)
    ang = np.arange(seq, dtype=np.float64)[:, None] * inv_freq[None, :]
    ang = np.concatenate([ang, ang], axis=-1)
    cos, sin = np.cos(ang), np.sin(ang)
    scale = (B_NOPE_DIM + B_ROPE_DIM) ** -0.5 * LOG2E
    one = np.ones((seq, B_NOPE_DIM))
    zn = np.zeros((seq, B_NOPE_DIM))
    zt = np.zeros((seq, LANES - B_NOPE_DIM - B_ROPE_DIM))
    cosq = np.concatenate([one, cos, zt], axis=1) * scale
    sinq = np.concatenate([zn, sin, zt], axis=1) * scale
    csk = np.concatenate([cos, sin, np.zeros((seq, LANES - 2 * B_ROPE_DIM))], axis=1)
    return {"csk": jnp.asarray(csk, F32), "cosq": jnp.asarray(cosq, F32), "sinq": jnp.asarray(sinq, F32)}


def kernel(x, w_in, q_norm_g, w_uq, kv_norm_g, w_ukv, w_o, ln_g, ln_b):
    bsz, seq, _ = x.shape
    wts = _layout_weights(w_in, q_norm_g, w_uq, kv_norm_g, w_ukv, w_o)
    tabs = _rope_tables(seq)
    aqkv1, aqkv4, aqkv16, gates, qf, kf, vt = _project(x, wts, tabs, tm=512)
    os_, lses = [], []
    for aqkv in (aqkv1[:, None], aqkv4, aqkv16):
        d, n = aqkv.shape[1], aqkv.shape[2]
        o, lse_t = _dilated(aqkv, tq=min(DILATED_TQ, n))
        os_.append(o)
        lses.append(jnp.transpose(lse_t, (0, 2, 3, 1)).reshape(bsz, A_HEADS, seq))
    os_[0] = os_[0][:, 0]
    b_out = _mla(qf, kf, vt, gates, tq=512, npair=4)
    return _finish(os_, lses, gates, b_out, x, wts["wo"], wts["expand"],
                   ln_g.reshape(1, -1).astype(F32), ln_b.reshape(1, -1).astype(F32), tm=512)
```

```python
import functools

import jax
import jax.numpy as jnp
import numpy as np
from jax import lax
from jax.experimental import pallas as pl
from jax.experimental.pallas import tpu as pltpu

D_MODEL = 1024
A_HEADS = 8
A_HEAD_DIM = 64
A_WIDTH = A_HEADS * A_HEAD_DIM
DILATED_PATTERNS = ((128, 1), (512, 4), (2048, 16))
DILATIONS = tuple(d for _, d in DILATED_PATTERNS)
BAND = 128
B_HEADS = 8
B_NOPE_DIM = 64
B_ROPE_DIM = 32
B_V_DIM = 64
B_WIDTH = B_HEADS * B_V_DIM
Q_LORA_RANK = 256
KV_LORA_RANK = 128
ROPE_THETA = 10000.0
LN_EPS = 1e-5
RMS_EPS = 1e-6
DEPTH = 1
DEEPNORM_ALPHA = (2 * DEPTH) ** 0.25

LANES = 128
SUBLANES = 8
HEAD_PAIRS = A_HEADS // 2
NEG = -1e30
VMEM_LIMIT = 56 * 1024 * 1024

BF16 = jnp.bfloat16
F32 = jnp.float32

C_AQKV = 0
C_AGATE = 3 * A_WIDTH
C_CQ = 4 * A_WIDTH
C_TAIL = C_CQ + Q_LORA_RANK + KV_LORA_RANK
IN_COLS = 3072
AQKV_SLABS = 3 * A_WIDTH // LANES

TN_DIMS = (((0,), (0,)), ((), ()))
LOG2E = float(np.log2(np.e))
ONES_ROWS = 16
ACC_ROWS = B_V_DIM + ONES_ROWS
KV_PER_STEP = 2
MLA_LOOKAHEAD = 2
OUT_CHUNK = 128
DILATED_GROUP = 8
DILATED_TQ = DILATED_GROUP * BAND


def _dot(a, b):
    return jnp.dot(a, b, preferred_element_type=F32)


def _dot_nt(a, b):
    return lax.dot_general(a, b, (((1,), (1,)), ((), ())), preferred_element_type=F32)


def _rms(t, g):
    return t * lax.rsqrt(jnp.mean(t * t, axis=-1, keepdims=True) + RMS_EPS) * g


def _silu(t):
    return t / (1.0 + jnp.exp(-t))


def _proj_kernel(x_ref, win_ref, wq_ref, wk_ref, wvt_ref, gq_ref, gkv_ref, csk_ref, cosq_ref, sinq_ref,
                 aqkv1_ref, aqkv4_ref, aqkv16_ref, gate_ref, qf_ref, kf_ref, vt_ref, slab_ref):
    xb = x_ref[...].astype(BF16)
    tm = xb.shape[0]
    lane_tile = lax.broadcasted_iota(jnp.int32, (1, LANES), 1)
    aqkv = _dot(xb, win_ref[:, C_AQKV:C_AGATE])
    aqkv = jnp.concatenate([aqkv[:, :A_WIDTH] * (A_HEAD_DIM ** -0.5 * LOG2E), aqkv[:, A_WIDTH:]], axis=1)
    aqkv1_ref[0] = aqkv.astype(BF16)
    for sl in range(AQKV_SLABS):
        slab_ref[sl] = aqkv[:, sl * LANES:(sl + 1) * LANES]
    for d, ref in ((DILATIONS[1], aqkv4_ref), (DILATIONS[2], aqkv16_ref)):
        for r in range(d):
            for sl in range(AQKV_SLABS):
                ref[r, :, sl * LANES:(sl + 1) * LANES] = slab_ref[sl, pl.ds(r, tm // d, stride=d), :].astype(BF16)
    rest = _dot(xb, win_ref[:, C_AGATE:IN_COLS])
    off = lambda c: c - C_AGATE
    gate_ref[:, 0:A_WIDTH] = _silu(rest[:, 0:off(C_CQ)]).astype(BF16)
    gate_ref[:, A_WIDTH:] = _silu(rest[:, off(C_TAIL):off(C_TAIL) + B_WIDTH]).astype(BF16)
    cq = rest[:, off(C_CQ):off(C_CQ) + Q_LORA_RANK]
    ckv = rest[:, off(C_CQ) + Q_LORA_RANK:off(C_TAIL)]
    rk = jnp.where(lane_tile < B_ROPE_DIM, rest[:, off(C_TAIL) + B_WIDTH:], 0.0)
    hr = B_ROPE_DIM // 2
    rope = (rk + jnp.where((lane_tile >= B_ROPE_DIM) & (lane_tile < B_ROPE_DIM + hr), -pltpu.roll(rk, hr, axis=1), 0.0)
            + jnp.where((lane_tile >= B_ROPE_DIM + hr) & (lane_tile < 2 * B_ROPE_DIM),
                        pltpu.roll(rk, B_ROPE_DIM + hr, axis=1), 0.0))
    q2 = _dot(_rms(cq, gq_ref[...]).astype(BF16), wq_ref[...])
    cosq, sinq = jnp.tile(cosq_ref[...], (1, B_HEADS)), jnp.tile(sinq_ref[...], (1, B_HEADS))
    qf_ref[...] = (q2[:, :B_HEADS * LANES] * cosq + q2[:, B_HEADS * LANES:] * sinq).astype(BF16)
    r2 = rope * csk_ref[...]
    kr = pltpu.roll(r2, B_NOPE_DIM, axis=1) + pltpu.roll(r2, B_NOPE_DIM - B_ROPE_DIM, axis=1)
    kr = jnp.where((lane_tile >= B_NOPE_DIM) & (lane_tile < B_NOPE_DIM + B_ROPE_DIM), kr, 0.0)
    kvn = _rms(ckv, gkv_ref[...]).astype(BF16)
    kn = _dot(kvn, wk_ref[...])
    for h in range(B_HEADS):
        kf_ref[:, h * LANES:(h + 1) * LANES] = (kn[:, h * LANES:(h + 1) * LANES] + kr).astype(BF16)
    vt_ref[...] = _dot_nt(wvt_ref[...], kvn).astype(BF16)


def _project(x, wts, tabs, tm):
    bsz, seq, _ = x.shape
    row = lambda w: pl.BlockSpec((None, tm, w), lambda i, b: (b, i, 0))
    res = lambda d: pl.BlockSpec((None, d, tm // d, 3 * A_WIDTH), lambda i, b: (b, 0, i, 0))
    full = lambda a: pl.BlockSpec(a.shape, lambda i, b: (0,) * a.ndim)
    tab = lambda w: pl.BlockSpec((tm, w), lambda i, b: (i, 0))
    out = lambda w: jax.ShapeDtypeStruct((bsz, seq, w), BF16)
    res_out = lambda d: jax.ShapeDtypeStruct((bsz, d, seq // d, 3 * A_WIDTH), BF16)
    return pl.pallas_call(
        _proj_kernel,
        grid=(seq // tm, bsz),
        in_specs=[row(D_MODEL), full(wts["win"]), full(wts["wq"]), full(wts["wk"]), full(wts["wvt"]),
                  full(wts["gq"]), full(wts["gkv"]),
                  tab(LANES), tab(LANES), tab(LANES)],
        out_specs=[res(DILATIONS[0]), res(DILATIONS[1]), res(DILATIONS[2]), row(A_WIDTH + B_WIDTH),
                   row(B_HEADS * LANES), row(B_HEADS * LANES),
                   pl.BlockSpec((None, B_WIDTH, tm), lambda i, b: (b, 0, i))],
        out_shape=[res_out(DILATIONS[0]), res_out(DILATIONS[1]), res_out(DILATIONS[2]), out(A_WIDTH + B_WIDTH),
                   out(B_HEADS * LANES), out(B_HEADS * LANES),
                   jax.ShapeDtypeStruct((bsz, B_WIDTH, seq), BF16)],
        scratch_shapes=[pltpu.VMEM((AQKV_SLABS, tm, LANES), F32)],
        compiler_params=pltpu.CompilerParams(dimension_semantics=("arbitrary", "arbitrary"),
                                             vmem_limit_bytes=VMEM_LIMIT),
        name="proj",
    )(x, wts["win"], wts["wq"], wts["wk"], wts["wvt"], wts["gq"], wts["gkv"],
      tabs["csk"], tabs["cosq"], tabs["sinq"])


def _dilated_kernel(q_ref, kc_ref, kp_ref, vc_ref, vp_ref, bias_ref, o_ref, lse_ref,
                    kk_ref, vt_ref, qq_ref, *, nsub, group):
    blk = pl.program_id(2)
    nres = q_ref.shape[0]
    lane = lax.broadcasted_iota(jnp.int32, (1, A_WIDTH), 1) % LANES
    lo = lane < A_HEAD_DIM
    for s in range(nres):
        for c in range(nsub):
            q = q_ref[s, c * BAND:(c + 1) * BAND, :]
            qq_ref[s, c, 0:BAND, :] = jnp.where(lo, q, jnp.zeros_like(q))
            qq_ref[s, c, BAND:, :] = jnp.where(lo, jnp.zeros_like(q), q)
        kk_ref[s, 0:BAND, :] = kp_ref[s]
        kk_ref[s, BAND:, :] = kc_ref[s]
        vt_ref[s, :, 0:BAND] = vp_ref[s].T
        vt_ref[s, :, BAND:] = vc_ref[s].T
    top = lax.broadcasted_iota(jnp.int32, (LANES, 1), 0) < A_HEAD_DIM
    first = (blk == 0).astype(jnp.int32)
    pair_cols = [slice(j * LANES, (j + 1) * LANES) for j in range(HEAD_PAIRS)]
    subs = [(s, c) for s in range(nres) for c in range(nsub)]
    for g0 in range(0, len(subs), group):
        blocks = [(s, c, j, slice(c * BAND, (c + 2) * BAND), pair_cols[j])
                  for s, c in subs[g0:g0 + group] for j in range(HEAD_PAIRS)]
        sts = [(_dot_nt(kk_ref[s, keys_rows, cols], qq_ref[s, c, :, cols])
                + bias_ref[first if c == 0 else 0, j]).astype(BF16)
               for s, c, j, keys_rows, cols in blocks]
        ms = [jnp.max(st, axis=0, keepdims=True) for st in sts]
        ps = [jnp.exp2(st - m) for st, m in zip(sts, ms)]
        ms = [m.astype(F32) for m in ms]
        ones = jnp.ones((ONES_ROWS, 2 * BAND), BF16)
        ot2s = [_dot(jnp.concatenate([vt_ref[s, cols, keys_rows], ones], axis=0), p)
                for (s, _, _, keys_rows, cols), p in zip(blocks, ps)]
        lses = {}
        for (s, c, j, _, cols), ot2, m in zip(blocks, ot2s, ms):
            l = ot2[LANES:LANES + 1, :]
            rl = 1.0 / l
            ot = jnp.where(top, ot2[0:LANES, 0:BAND] * rl[:, 0:BAND], ot2[0:LANES, BAND:] * rl[:, BAND:])
            o_ref[s, c * BAND:(c + 1) * BAND, cols] = ot.T.astype(BF16)
            lse = m + jnp.log2(l)
            lses.setdefault((s, c), []).extend([lse[:, 0:BAND], lse[:, BAND:]])
        for (s, c), rows in lses.items():
            lse_ref[s, :, c * BAND:(c + 1) * BAND] = jnp.concatenate(rows, axis=0)


def _dilated_bias(dilation):
    qi = np.arange(BAND)[None, :]
    kj = np.arange(2 * BAND)[:, None]
    rel = qi + BAND - kj
    valid = (rel >= 0) & (rel <= BAND)
    slopes = 2.0 ** (-8.0 * np.arange(1, A_HEADS + 1, dtype=np.float64) / A_HEADS)
    bias = -slopes[:, None, None] * (rel * dilation).astype(np.float64)[None] * LOG2E
    gen = np.where(valid[None], bias, NEG)
    fst = np.where((valid & (kj >= BAND))[None], bias, NEG)
    pair = lambda t: np.concatenate([t[0::2], t[1::2]], axis=-1)
    return jnp.asarray(np.stack([pair(gen), pair(fst)]), F32)


def _dilated(aqkv, tq):
    bsz, dilation, n, _ = aqkv.shape
    nsub = tq // BAND
    nres = min(dilation, DILATED_GROUP // nsub)
    cur = lambda which: pl.BlockSpec((None, nres, tq, A_WIDTH), lambda b, r, i: (b, r, i, which))
    prev = lambda which: pl.BlockSpec((None, nres, BAND, A_WIDTH),
                                      lambda b, r, i: (b, r, jnp.maximum(i * nsub - 1, 0), which))
    bias = _dilated_bias(dilation)
    return pl.pallas_call(
        functools.partial(_dilated_kernel, nsub=nsub, group=DILATED_GROUP),
        grid=(bsz, dilation // nres, n // tq),
        in_specs=[cur(0), cur(1), prev(1), cur(2), prev(2),
                  pl.BlockSpec(bias.shape, lambda b, r, i: (0, 0, 0, 0))],
        out_specs=[pl.BlockSpec((None, nres, tq, A_WIDTH), lambda b, r, i: (b, r, i, 0)),
                   pl.BlockSpec((None, nres, A_HEADS, tq), lambda b, r, i: (b, r, 0, i))],
        out_shape=[jax.ShapeDtypeStruct((bsz, dilation, n, A_WIDTH), BF16),
                   jax.ShapeDtypeStruct((bsz, dilation, A_HEADS, n), F32)],
        scratch_shapes=[pltpu.VMEM((nres, tq + BAND, A_WIDTH), BF16), pltpu.VMEM((nres, A_WIDTH, tq + BAND), BF16),
                        pltpu.VMEM((nres, nsub, 2 * BAND, A_WIDTH), BF16)],
        compiler_params=pltpu.CompilerParams(dimension_semantics=("arbitrary",) * 3,
                                             vmem_limit_bytes=VMEM_LIMIT),
        name=f"dilated_d{dilation}",
    )(aqkv, aqkv, aqkv, aqkv, aqkv, bias)


def _mla_kernel(qi_ref, kj_ref, q_ref, k_ref, vt_ref, g_ref, o_ref, m_ref, acc_ref, *, npair):
    t = pl.program_id(2)
    qi, kj = qi_ref[t], kj_ref[t]
    heads = range(2 * npair)
    acc_rows = lambda h: slice(h * ACC_ROWS, (h + 1) * ACC_ROWS)

    @pl.when(kj == 0)
    def _():
        m_ref[...] = jnp.full(m_ref.shape, NEG, F32)
        acc_ref[...] = jnp.zeros(acc_ref.shape, F32)

    tq = q_ref.shape[0]
    half = tq // 2

    def scores(h, blocks):
        cols = slice(h * LANES, (h + 1) * LANES)
        chunks = []
        for blk, diagonal in blocks:
            k0 = blk * tq
            if not diagonal:
                chunks.append((slice(k0, k0 + tq), 0, _dot_nt(k_ref[k0:k0 + tq, cols], q_ref[:, cols]).astype(BF16)))
                continue
            tri = (lax.broadcasted_iota(jnp.int32, (half, half), 0)
                   <= lax.broadcasted_iota(jnp.int32, (half, half), 1))
            top = _dot_nt(k_ref[k0:k0 + half, cols], q_ref[:, cols])
            top = jnp.concatenate([jnp.where(tri, top[:, 0:half], NEG), top[:, half:]], axis=1)
            bot = jnp.where(tri, _dot_nt(k_ref[k0 + half:k0 + tq, cols], q_ref[half:, cols]), NEG)
            chunks.append((slice(k0, k0 + half), 0, top.astype(BF16)))
            chunks.append((slice(k0 + half, k0 + tq), half, bot.astype(BF16)))
        return chunks

    def accumulate(h, chunks):
        v_ext = lambda keys: jnp.concatenate([vt_ref[h * B_V_DIM:(h + 1) * B_V_DIM, keys],
                                              jnp.ones((ONES_ROWS, keys.stop - keys.start), BF16)], axis=0)
        widen = lambda a, q0, fill: a if q0 == 0 else jnp.concatenate(
            [jnp.full((a.shape[0], q0), fill, a.dtype), a], axis=1)
        m_old = m_ref[h]
        m_new = m_old
        for _, q0, st in chunks:
            m_new = jnp.maximum(m_new, widen(jnp.max(st, axis=0, keepdims=True).astype(F32), q0, NEG))
        m_b = m_new.astype(BF16)
        pv = None
        for keys, q0, st in chunks:
            part = widen(_dot(v_ext(keys), jnp.exp2(st - m_b[:, q0:])), q0, 0.0)
            pv = part if pv is None else pv + part
        alpha = jnp.exp2(m_old - m_new)
        m_ref[h] = m_new
        acc_ref[acc_rows(h), :] = alpha * acc_ref[acc_rows(h), :] + pv

    def step(blocks):
        pending = {}
        for i in range(len(heads) + MLA_LOOKAHEAD):
            if i < len(heads):
                pending[i] = scores(heads[i], blocks)
            if i >= MLA_LOOKAHEAD:
                accumulate(heads[i - MLA_LOOKAHEAD], pending.pop(i - MLA_LOOKAHEAD))

    last = kj == qi // KV_PER_STEP
    odd = qi % KV_PER_STEP == 1

    @pl.when(jnp.logical_not(last))
    def _():
        step([(0, False), (1, False)])

    @pl.when(jnp.logical_and(last, jnp.logical_not(odd)))
    def _():
        step([(0, True)])

    @pl.when(jnp.logical_and(last, odd))
    def _():
        step([(0, False), (1, True)])

    @pl.when(last)
    def _():
        ot = jnp.concatenate([acc_ref[h * ACC_ROWS:h * ACC_ROWS + B_V_DIM, :]
                              * (1.0 / acc_ref[h * ACC_ROWS + B_V_DIM:h * ACC_ROWS + B_V_DIM + 1, :])
                              for h in heads], axis=0)
        o_ref[...] = (ot.T * g_ref[...].astype(F32)).astype(BF16)


def _mla(qf, kf, vt, gates, tq, npair):
    bsz, seq, _ = qf.shape
    nq = seq // tq
    ngrp = HEAD_PAIRS // npair
    steps = [(i, j) for i in range(nq) for j in range(i // KV_PER_STEP + 1)]
    qi = np.asarray([i for i, _ in steps], np.int32)
    kj = np.asarray([j for _, j in steps], np.int32)
    wide, narrow, tk = 2 * npair * LANES, npair * LANES, KV_PER_STEP * tq
    grid_spec = pltpu.PrefetchScalarGridSpec(
        num_scalar_prefetch=2,
        grid=(bsz, ngrp, len(steps)),
        in_specs=[pl.BlockSpec((None, tq, wide), lambda b, p, t, qi, kj: (b, qi[t], p)),
                  pl.BlockSpec((None, tk, wide), lambda b, p, t, qi, kj: (b, kj[t], p)),
                  pl.BlockSpec((None, narrow, tk), lambda b, p, t, qi, kj: (b, p, kj[t])),
                  pl.BlockSpec((None, tq, narrow), lambda b, p, t, qi, kj: (b, qi[t], ngrp + p))],
        out_specs=pl.BlockSpec((None, tq, narrow), lambda b, p, t, qi, kj: (b, qi[t], p)),
        scratch_shapes=[pltpu.VMEM((2 * npair, 1, tq), F32), pltpu.VMEM((2 * npair * ACC_ROWS, tq), F32)],
    )
    return pl.pallas_call(
        functools.partial(_mla_kernel, npair=npair),
        grid_spec=grid_spec,
        out_shape=jax.ShapeDtypeStruct((bsz, seq, B_WIDTH), BF16),
        compiler_params=pltpu.CompilerParams(dimension_semantics=("arbitrary",) * 3,
                                             vmem_limit_bytes=VMEM_LIMIT),
        name="mla",
    )(jnp.asarray(qi), jnp.asarray(kj), qf, kf, vt, gates)


def _out_kernel(o1_ref, o4_ref, o16_ref, l1_ref, l4_ref, l16_ref, gate_ref, b_ref, x_ref,
                wo_ref, expand_ref, lng_ref, lnb_ref, out_ref, slab4_ref, slab16_ref):
    tm = out_ref.shape[0]
    nslab = A_WIDTH // LANES

    def unperm(o_ref, slab_ref):
        d = o_ref.shape[0]
        for r in range(d):
            o = o_ref[r].astype(F32)
            for sl in range(nslab):
                slab_ref[sl, pl.ds(r, tm // d, stride=d), :] = o[:, sl * LANES:(sl + 1) * LANES]

    unperm(o4_ref, slab4_ref)
    unperm(o16_ref, slab16_ref)
    l1, l4, l16 = l1_ref[...], l4_ref[...], l16_ref[...]
    m = jnp.maximum(jnp.maximum(l1, l4), l16)
    e1, e4, e16 = jnp.exp2(l1 - m), jnp.exp2(l4 - m), jnp.exp2(l16 - m)
    rden = 1.0 / (e1 + e4 + e16)
    pad = jnp.zeros((2 * SUBLANES - A_HEADS, tm), F32)
    wts = [jnp.concatenate([e * rden, pad], axis=0).astype(BF16) for e in (e1, e4, e16)]

    def merge(rows):
        expand = lambda wt: lax.dot_general(wt[:, rows], expand_ref[...], TN_DIMS, preferred_element_type=F32)
        slab = lambda ref: jnp.concatenate([ref[sl, rows, :] for sl in range(nslab)], axis=1)
        a = expand(wts[0]) * o1_ref[0, rows, :].astype(F32)
        a = a + expand(wts[1]) * slab(slab4_ref) + expand(wts[2]) * slab(slab16_ref)
        return (a * gate_ref[rows, 0:A_WIDTH].astype(F32)).astype(BF16)

    def project(rows, a):
        return _dot(a, wo_ref[0:A_WIDTH, :]) + _dot(b_ref[rows, :], wo_ref[A_WIDTH:, :])

    def norm(rows, y):
        z = DEEPNORM_ALPHA * x_ref[rows, :] + y
        mu = jnp.mean(z, axis=-1, keepdims=True)
        zc = z - mu
        var = jnp.mean(zc * zc, axis=-1, keepdims=True)
        out_ref[rows, :] = zc * lax.rsqrt(var + LN_EPS) * lng_ref[...] + lnb_ref[...]

    chunks = [slice(k * OUT_CHUNK, (k + 1) * OUT_CHUNK) for k in range(tm // OUT_CHUNK)]
    merged, projected = {}, {}
    for k in range(len(chunks) + 2):
        if k < len(chunks):
            merged[k] = merge(chunks[k])
        if 1 <= k <= len(chunks):
            projected[k - 1] = project(chunks[k - 1], merged.pop(k - 1))
        if k >= 2:
            norm(chunks[k - 2], projected.pop(k - 2))


def _finish(os_, lses, gates, b_out, x, wo, expand, ln_g, ln_b, tm):
    bsz, seq, _ = x.shape
    row = lambda w: pl.BlockSpec((None, tm, w), lambda b, i: (b, i, 0))
    res = lambda d: pl.BlockSpec((None, d, tm // d, A_WIDTH), lambda b, i: (b, 0, i, 0))
    lse = pl.BlockSpec((None, A_HEADS, tm), lambda b, i: (b, 0, i))
    full = lambda a: pl.BlockSpec(a.shape, lambda b, i: (0,) * a.ndim)
    return pl.pallas_call(
        _out_kernel,
        grid=(bsz, seq // tm),
        in_specs=[res(DILATIONS[0]), res(DILATIONS[1]), res(DILATIONS[2]),
                  lse, lse, lse, row(A_WIDTH), row(B_WIDTH), row(D_MODEL),
                  full(wo), full(expand), full(ln_g), full(ln_b)],
        out_specs=row(D_MODEL),
        out_shape=jax.ShapeDtypeStruct((bsz, seq, D_MODEL), F32),
        scratch_shapes=[pltpu.VMEM((A_WIDTH // LANES, tm, LANES), F32)] * 2,
        compiler_params=pltpu.CompilerParams(dimension_semantics=("arbitrary", "arbitrary"),
                                             vmem_limit_bytes=VMEM_LIMIT),
        name="merge_out_ln",
    )(*os_, *lses, gates, b_out, x, wo, expand, ln_g, ln_b)


def _rot_cols(w):
    half = w.shape[-1] // 2
    return jnp.concatenate([-w[..., half:], w[..., :half]], axis=-1)


def _layout_weights(w_in, q_norm_g, w_uq, kv_norm_g, w_ukv, w_o):
    head, k_rope, b_gate = w_in[:, :C_TAIL], w_in[:, C_TAIL:C_TAIL + B_ROPE_DIM], w_in[:, C_TAIL + B_ROPE_DIM:]
    win = jnp.pad(jnp.concatenate([head, b_gate, k_rope], axis=1).astype(BF16),
                  ((0, 0), (0, IN_COLS - w_in.shape[1])))
    uq =w_uq.reshape(Q_LORA_RANK, B_HEADS, B_NOPE_DIM + B_ROPE_DIM)
    nope, rope = uq[..., :B_NOPE_DIM], uq[..., B_NOPE_DIM:]
    z = lambda w: jnp.zeros((Q_LORA_RANK, B_HEADS, w), w_uq.dtype)
    tail = LANES - B_NOPE_DIM - B_ROPE_DIM
    w1 = jnp.concatenate([nope, rope, z(tail)], axis=-1).reshape(Q_LORA_RANK, B_HEADS * LANES)
    w2 = jnp.concatenate([z(B_NOPE_DIM), _rot_cols(rope), z(tail)], axis=-1).reshape(Q_LORA_RANK, B_HEADS * LANES)
    wq = jnp.concatenate([w1, w2], axis=1).astype(BF16)
    ukv = w_ukv.reshape(KV_LORA_RANK, B_HEADS, B_NOPE_DIM + B_V_DIM)
    wk = jnp.concatenate([ukv[..., :B_NOPE_DIM], jnp.zeros((KV_LORA_RANK, B_HEADS, LANES - B_NOPE_DIM), w_ukv.dtype)],
                         axis=-1).reshape(KV_LORA_RANK, B_HEADS * LANES)
    wvt = ukv[..., B_NOPE_DIM:].reshape(KV_LORA_RANK, B_WIDTH).T
    expand =np.zeros((2 * SUBLANES, A_WIDTH), np.float32)
    for h in range(A_HEADS):
        expand[h, h * A_HEAD_DIM:(h + 1) * A_HEAD_DIM] = 1.0
    return {"win": win, "wq": wq, "wk": wk.astype(BF16), "wvt": wvt.astype(BF16),
            "gq": q_norm_g.reshape(1, -1).astype(F32), "gkv": kv_norm_g.reshape(1, -1).astype(F32),
            "wo": w_o.astype(BF16), "expand": jnp.asarray(expand, BF16)}


def _rope_tables(seq):
    inv_freq = ROPE_THETA ** (-np.arange(0, B_ROPE_DIM, 2, dtype=np.float64) / B_ROPE_DIM)
    ang = np.arange(seq, dtype=np.float64)[:, None] * inv_freq[None, :]
    ang = np.concatenate([ang, ang], axis=-1)
    cos, sin = np.cos(ang), np.sin(ang)
    scale = (B_NOPE_DIM + B_ROPE_DIM) ** -0.5 * LOG2E
    one = np.ones((seq, B_NOPE_DIM))
    zn = np.zeros((seq, B_NOPE_DIM))
    zt = np.zeros((seq, LANES - B_NOPE_DIM - B_ROPE_DIM))
    cosq = np.concatenate([one, cos, zt], axis=1) * scale
    sinq = np.concatenate([zn, sin, zt], axis=1) * scale
    csk = np.concatenate([cos, sin, np.zeros((seq, LANES - 2 * B_ROPE_DIM))], axis=1)
    return {"csk": jnp.asarray(csk, F32), "cosq": jnp.asarray(cosq, F32), "sinq": jnp.asarray(sinq, F32)}


def kernel(x, w_in, q_norm_g, w_uq, kv_norm_g, w_ukv, w_o, ln_g, ln_b):
    bsz, seq, _ = x.shape
    wts = _layout_weights(w_in, q_norm_g, w_uq, kv_norm_g, w_ukv, w_o)
    tabs = _rope_tables(seq)
    aqkv1, aqkv4, aqkv16, gates, qf, kf, vt = _project(x, wts, tabs, tm=512)
    os_, lses = [], []
    for aqkv in (aqkv1, aqkv4, aqkv16):
        d, n = aqkv.shape[1], aqkv.shape[2]
        o, lse_t = _dilated(aqkv, tq=min(DILATED_TQ, n))
        os_.append(o)
        lses.append(lse_t.reshape(bsz, A_HEADS, seq) if d == 1 else
                    jnp.transpose(lse_t, (0, 2, 3, 1)).reshape(bsz, A_HEADS, seq))
    b_out = _mla(qf, kf, vt, gates, tq=512, npair=4)
    return _finish(os_, lses, gates, b_out, x, wts["wo"], wts["expand"],
                   ln_g.reshape(1, -1).astype(F32), ln_b.reshape(1, -1).astype(F32), tm=512)
```

```python
import functools

import jax
import jax.numpy as jnp
import numpy as np
from jax import lax
from jax.experimental import pallas as pl
from jax.experimental.pallas import tpu as pltpu

D_MODEL = 1024
A_HEADS = 8
A_HEAD_DIM = 64
A_WIDTH = A_HEADS * A_HEAD_DIM
DILATED_PATTERNS = ((128, 1), (512, 4), (2048, 16))
DILATIONS = tuple(d for _, d in DILATED_PATTERNS)
BAND = 128
B_HEADS = 8
B_NOPE_DIM = 64
B_ROPE_DIM = 32
B_V_DIM = 64
B_WIDTH = B_HEADS * B_V_DIM
Q_LORA_RANK = 256
KV_LORA_RANK = 128
ROPE_THETA = 10000.0
LN_EPS = 1e-5
RMS_EPS = 1e-6
DEPTH = 1
DEEPNORM_ALPHA = (2 * DEPTH) ** 0.25

LANES = 128
SUBLANES = 8
HEAD_PAIRS = A_HEADS // 2
NEG = -1e30
VMEM_LIMIT = 56 * 1024 * 1024

BF16 = jnp.bfloat16
F32 = jnp.float32

C_AQKV = 0
C_AGATE = 3 * A_WIDTH
C_CQ = 4 * A_WIDTH
C_TAIL = C_CQ + Q_LORA_RANK + KV_LORA_RANK
IN_COLS = 3072
AQKV_SLABS = 3 * A_WIDTH // LANES

TN_DIMS = (((0,), (0,)), ((), ()))
LOG2E = float(np.log2(np.e))
ONES_ROWS = 16
ACC_ROWS = B_V_DIM + ONES_ROWS
KV_PER_STEP = 2
MLA_LOOKAHEAD = 2
OUT_CHUNK = 128
DILATED_GROUP = 8
DILATED_TQ = DILATED_GROUP * BAND


def _dot(a, b):
    return jnp.dot(a, b, preferred_element_type=F32)


def _dot_nt(a, b):
    return lax.dot_general(a, b, (((1,), (1,)), ((), ())), preferred_element_type=F32)


def _rms(t, g):
    return t * lax.rsqrt(jnp.mean(t * t, axis=-1, keepdims=True) + RMS_EPS) * g


def _silu(t):
    return t / (1.0 + jnp.exp(-t))


def _proj_kernel(x_ref, win_ref, wq_ref, wk_ref, wvt_ref, gq_ref, gkv_ref, csk_ref, cosq_ref, sinq_ref,
                 aqkv1_ref, aqkv4_ref, aqkv16_ref, gate_ref, qf_ref, kf_ref, vt_ref, slab_ref):
    xb = x_ref[...].astype(BF16)
    tm = xb.shape[0]
    lane_tile = lax.broadcasted_iota(jnp.int32, (1, LANES), 1)
    aqkv = _dot(xb, win_ref[:, C_AQKV:C_AGATE])
    aqkv = jnp.concatenate([aqkv[:, :A_WIDTH] * (A_HEAD_DIM ** -0.5 * LOG2E), aqkv[:, A_WIDTH:]], axis=1)
    aqkv1_ref[0] = aqkv.astype(BF16)
    for sl in range(AQKV_SLABS):
        slab_ref[sl] = aqkv[:, sl * LANES:(sl + 1) * LANES]
    for d, ref in ((DILATIONS[1], aqkv4_ref), (DILATIONS[2], aqkv16_ref)):
        for r in range(d):
            for sl in range(AQKV_SLABS):
                ref[r, :, sl * LANES:(sl + 1) * LANES] = slab_ref[sl, pl.ds(r, tm // d, stride=d), :].astype(BF16)
    rest = _dot(xb, win_ref[:, C_AGATE:IN_COLS])
    off = lambda c: c - C_AGATE
    gate_ref[:, 0:A_WIDTH] = _silu(rest[:, 0:off(C_CQ)]).astype(BF16)
    gate_ref[:, A_WIDTH:] = _silu(rest[:, off(C_TAIL):off(C_TAIL) + B_WIDTH]).astype(BF16)
    cq = rest[:, off(C_CQ):off(C_CQ) + Q_LORA_RANK]
    ckv = rest[:, off(C_CQ) + Q_LORA_RANK:off(C_TAIL)]
    rk = jnp.where(lane_tile < B_ROPE_DIM, rest[:, off(C_TAIL) + B_WIDTH:], 0.0)
    hr = B_ROPE_DIM // 2
    rope = (rk + jnp.where((lane_tile >= B_ROPE_DIM) & (lane_tile < B_ROPE_DIM + hr), -pltpu.roll(rk, hr, axis=1), 0.0)
            + jnp.where((lane_tile >= B_ROPE_DIM + hr) & (lane_tile < 2 * B_ROPE_DIM),
                        pltpu.roll(rk, B_ROPE_DIM + hr, axis=1), 0.0))
    q2 = _dot(_rms(cq, gq_ref[...]).astype(BF16), wq_ref[...])
    cosq, sinq = jnp.tile(cosq_ref[...], (1, B_HEADS)), jnp.tile(sinq_ref[...], (1, B_HEADS))
    qf_ref[...] = (q2[:, :B_HEADS * LANES] * cosq + q2[:, B_HEADS * LANES:] * sinq).astype(BF16)
    r2 = rope * csk_ref[...]
    kr = pltpu.roll(r2, B_NOPE_DIM, axis=1) + pltpu.roll(r2, B_NOPE_DIM - B_ROPE_DIM, axis=1)
    kr = jnp.where((lane_tile >= B_NOPE_DIM) & (lane_tile < B_NOPE_DIM + B_ROPE_DIM), kr, 0.0)
    kvn = _rms(ckv, gkv_ref[...]).astype(BF16)
    kn = _dot(kvn, wk_ref[...])
    for h in range(B_HEADS):
        kf_ref[:, h * LANES:(h + 1) * LANES] = (kn[:, h * LANES:(h + 1) * LANES] + kr).astype(BF16)
    vt_ref[...] = _dot_nt(wvt_ref[...], kvn).astype(BF16)


def _project(x, wts, tabs, tm):
    bsz, seq, _ = x.shape
    row = lambda w: pl.BlockSpec((None, tm, w), lambda i, b: (b, i, 0))
    res = lambda d: pl.BlockSpec((None, d, tm // d, 3 * A_WIDTH), lambda i, b: (b, 0, i, 0))
    full = lambda a: pl.BlockSpec(a.shape, lambda i, b: (0,) * a.ndim)
    tab = lambda w: pl.BlockSpec((tm, w), lambda i, b: (i, 0))
    out = lambda w: jax.ShapeDtypeStruct((bsz, seq, w), BF16)
    res_out = lambda d: jax.ShapeDtypeStruct((bsz, d, seq // d, 3 * A_WIDTH), BF16)
    return pl.pallas_call(
        _proj_kernel,
        grid=(seq // tm, bsz),
        in_specs=[row(D_MODEL), full(wts["win"]), full(wts["wq"]), full(wts["wk"]), full(wts["wvt"]),
                  full(wts["gq"]), full(wts["gkv"]),
                  tab(LANES), tab(LANES), tab(LANES)],
        out_specs=[res(DILATIONS[0]), res(DILATIONS[1]), res(DILATIONS[2]), row(A_WIDTH + B_WIDTH),
                   row(B_HEADS * LANES), row(B_HEADS * LANES),
                   pl.BlockSpec((None, B_WIDTH, tm), lambda i, b: (b, 0, i))],
        out_shape=[res_out(DILATIONS[0]), res_out(DILATIONS[1]), res_out(DILATIONS[2]), out(A_WIDTH + B_WIDTH),
                   out(B_HEADS * LANES), out(B_HEADS * LANES),
                   jax.ShapeDtypeStruct((bsz, B_WIDTH, seq), BF16)],
        scratch_shapes=[pltpu.VMEM((AQKV_SLABS, tm, LANES), F32)],
        compiler_params=pltpu.CompilerParams(dimension_semantics=("arbitrary", "arbitrary"),
                                             vmem_limit_bytes=VMEM_LIMIT),
        name="proj",
    )(x, wts["win"], wts["wq"], wts["wk"], wts["wvt"], wts["gq"], wts["gkv"],
      tabs["csk"], tabs["cosq"], tabs["sinq"])


def _dilated_kernel(q_ref, kc_ref, kp_ref, vc_ref, vp_ref, bias_ref, o_ref, lse_ref,
                    kk_ref, vt_ref, qq_ref, *, nsub, group):
    blk = pl.program_id(2)
    nres = q_ref.shape[0]
    lane = lax.broadcasted_iota(jnp.int32, (1, A_WIDTH), 1) % LANES
    lo = lane < A_HEAD_DIM
    for s in range(nres):
        for c in range(nsub):
            q = q_ref[s, c * BAND:(c + 1) * BAND, :]
            qq_ref[s, c, 0:BAND, :] = jnp.where(lo, q, jnp.zeros_like(q))
            qq_ref[s, c, BAND:, :] = jnp.where(lo, jnp.zeros_like(q), q)
        kk_ref[s, 0:BAND, :] = kp_ref[s]
        kk_ref[s, BAND:, :] = kc_ref[s]
        vt_ref[s, :, 0:BAND] = vp_ref[s].T
        vt_ref[s, :, BAND:] = vc_ref[s].T
    top = lax.broadcasted_iota(jnp.int32, (LANES, 1), 0) < A_HEAD_DIM
    first = (blk == 0).astype(jnp.int32)
    pair_cols = [slice(j * LANES, (j + 1) * LANES) for j in range(HEAD_PAIRS)]
    subs = [(s, c) for s in range(nres) for c in range(nsub)]
    for g0 in range(0, len(subs), group):
        blocks = [(s, c, j, slice(c * BAND, (c + 2) * BAND), pair_cols[j])
                  for s, c in subs[g0:g0 + group] for j in range(HEAD_PAIRS)]
        sts = [(_dot_nt(kk_ref[s, keys_rows, cols], qq_ref[s, c, :, cols])
                + bias_ref[first if c == 0 else 0, j]).astype(BF16)
               for s, c, j, keys_rows, cols in blocks]
        ms = [jnp.max(st, axis=0, keepdims=True) for st in sts]
        ps = [jnp.exp2(st - m) for st, m in zip(sts, ms)]
        ms = [m.astype(F32) for m in ms]
        ones = jnp.ones((ONES_ROWS, 2 * BAND), BF16)
        ot2s = [_dot(jnp.concatenate([vt_ref[s, cols, keys_rows], ones], axis=0), p)
                for (s, _, _, keys_rows, cols), p in zip(blocks, ps)]
        lses = {}
        for (s, c, j, _, cols), ot2, m in zip(blocks, ot2s, ms):
            l = ot2[LANES:LANES + 1, :]
            rl = 1.0 / l
            ot = jnp.where(top, ot2[0:LANES, 0:BAND] * rl[:, 0:BAND], ot2[0:LANES, BAND:] * rl[:, BAND:])
            o_ref[s, c * BAND:(c + 1) * BAND, cols] = ot.T.astype(BF16)
            lse = m + jnp.log2(l)
            lses.setdefault((s, c), []).extend([lse[:, 0:BAND], lse[:, BAND:]])
        for (s, c), rows in lses.items():
            lse_ref[s, :, c * BAND:(c + 1) * BAND] = jnp.concatenate(rows, axis=0)


def _dilated_bias(dilation):
    qi = np.arange(BAND)[None, :]
    kj = np.arange(2 * BAND)[:, None]
    rel = qi + BAND - kj
    valid = (rel >= 0) & (rel <= BAND)
    slopes = 2.0 ** (-8.0 * np.arange(1, A_HEADS + 1, dtype=np.float64) / A_HEADS)
    bias = -slopes[:, None, None] * (rel * dilation).astype(np.float64)[None] * LOG2E
    gen = np.where(valid[None], bias, NEG)
    fst = np.where((valid & (kj >= BAND))[None], bias, NEG)
    pair = lambda t: np.concatenate([t[0::2], t[1::2]], axis=-1)
    return jnp.asarray(np.stack([pair(gen), pair(fst)]), F32)


def _dilated(aqkv, tq):
    bsz, dilation, n, _ = aqkv.shape
    nsub = tq // BAND
    nres = min(dilation, DILATED_GROUP // nsub)
    cur = lambda which: pl.BlockSpec((None, nres, tq, A_WIDTH), lambda b, r, i: (b, r, i, which))
    prev = lambda which: pl.BlockSpec((None, nres, BAND, A_WIDTH),
                                      lambda b, r, i: (b, r, jnp.maximum(i * nsub - 1, 0), which))
    bias = _dilated_bias(dilation)
    return pl.pallas_call(
        functools.partial(_dilated_kernel, nsub=nsub, group=DILATED_GROUP),
        grid=(bsz, dilation // nres, n // tq),
        in_specs=[cur(0), cur(1), prev(1), cur(2), prev(2),
                  pl.BlockSpec(bias.shape, lambda b, r, i: (0, 0, 0, 0))],
        out_specs=[pl.BlockSpec((None, nres, tq, A_WIDTH), lambda b, r, i: (b, r, i, 0)),
                   pl.BlockSpec((None, nres, A_HEADS, tq), lambda b, r, i: (b, r, 0, i))],
        out_shape=[jax.ShapeDtypeStruct((bsz, dilation, n, A_WIDTH), BF16),
                   jax.ShapeDtypeStruct((bsz, dilation, A_HEADS, n), F32)],
        scratch_shapes=[pltpu.VMEM((nres, tq + BAND, A_WIDTH), BF16), pltpu.VMEM((nres, A_WIDTH, tq + BAND), BF16),
                        pltpu.VMEM((nres, nsub, 2 * BAND, A_WIDTH), BF16)],
        compiler_params=pltpu.CompilerParams(dimension_semantics=("arbitrary",) * 3,
                                             vmem_limit_bytes=VMEM_LIMIT),
        name=f"dilated_d{dilation}",
    )(aqkv, aqkv, aqkv, aqkv, aqkv, bias)


def _mla_kernel(qi_ref, kj_ref, q_ref, k_ref, vt_ref, g_ref, o_ref, m_ref, acc_ref, *, npair):
    t = pl.program_id(2)
    qi, kj = qi_ref[t], kj_ref[t]
    heads = range(2 * npair)
    acc_rows = lambda h: slice(h * ACC_ROWS, (h + 1) * ACC_ROWS)

    @pl.when(kj == 0)
    def _():
        m_ref[...] = jnp.full(m_ref.shape, NEG, F32)
        acc_ref[...] = jnp.zeros(acc_ref.shape, F32)

    tq = q_ref.shape[0]
    half = tq // 2

    def scores(h, blocks):
        cols = slice(h * LANES, (h + 1) * LANES)
        chunks = []
        for blk, diagonal in blocks:
            k0 = blk * tq
            if not diagonal:
                chunks.append((slice(k0, k0 + tq), 0, _dot_nt(k_ref[k0:k0 + tq, cols], q_ref[:, cols]).astype(BF16)))
                continue
            tri = (lax.broadcasted_iota(jnp.int32, (half, half), 0)
                   <= lax.broadcasted_iota(jnp.int32, (half, half), 1))
            top = _dot_nt(k_ref[k0:k0 + half, cols], q_ref[:, cols])
            top = jnp.concatenate([jnp.where(tri, top[:, 0:half], NEG), top[:, half:]], axis=1)
            bot = jnp.where(tri, _dot_nt(k_ref[k0 + half:k0 + tq, cols], q_ref[half:, cols]), NEG)
            chunks.append((slice(k0, k0 + half), 0, top.astype(BF16)))
            chunks.append((slice(k0 + half, k0 + tq), half, bot.astype(BF16)))
        return chunks

    def accumulate(h, chunks):
        v_ext = lambda keys: jnp.concatenate([vt_ref[h * B_V_DIM:(h + 1) * B_V_DIM, keys],
                                              jnp.ones((ONES_ROWS, keys.stop - keys.start), BF16)], axis=0)
        widen = lambda a, q0, fill: a if q0 == 0 else jnp.concatenate(
            [jnp.full((a.shape[0], q0), fill, a.dtype), a], axis=1)
        m_old = m_ref[h]
        m_new = m_old
        for _, q0, st in chunks:
            m_new = jnp.maximum(m_new, widen(jnp.max(st, axis=0, keepdims=True).astype(F32), q0, NEG))
        m_b = m_new.astype(BF16)
        pv = None
        for keys, q0, st in chunks:
            part = widen(_dot(v_ext(keys), jnp.exp2(st - m_b[:, q0:])), q0, 0.0)
            pv = part if pv is None else pv + part
        alpha = jnp.exp2(m_old - m_new)
        m_ref[h] = m_new
        acc_ref[acc_rows(h), :] = alpha * acc_ref[acc_rows(h), :] + pv

    def step(blocks):
        pending = {}
        for i in range(len(heads) + MLA_LOOKAHEAD):
            if i < len(heads):
                pending[i] = scores(heads[i], blocks)
            if i >= MLA_LOOKAHEAD:
                accumulate(heads[i - MLA_LOOKAHEAD], pending.pop(i - MLA_LOOKAHEAD))

    last = kj == qi // KV_PER_STEP
    odd = qi % KV_PER_STEP == 1

    @pl.when(jnp.logical_not(last))
    def _():
        step([(0, False), (1, False)])

    @pl.when(jnp.logical_and(last, jnp.logical_not(odd)))
    def _():
        step([(0, True)])

    @pl.when(jnp.logical_and(last, odd))
    def _():
        step([(0, False), (1, True)])

    @pl.when(last)
    def _():
        ot = jnp.concatenate([acc_ref[h * ACC_ROWS:h * ACC_ROWS + B_V_DIM, :]
                              * (1.0 / acc_ref[h * ACC_ROWS + B_V_DIM:h * ACC_ROWS + B_V_DIM + 1, :])
                              for h in heads], axis=0)
        o_ref[...] = (ot.T * g_ref[...].astype(F32)).astype(BF16)


def _mla(qf, kf, vt, gates, tq, npair):
    bsz, seq, _ = qf.shape
    nq = seq // tq
    ngrp = HEAD_PAIRS // npair
    steps = [(i, j) for i in range(nq) for j in range(i // KV_PER_STEP + 1)]
    qi = np.asarray([i for i, _ in steps], np.int32)
    kj = np.asarray([j for _, j in steps], np.int32)
    wide, narrow, tk = 2 * npair * LANES, npair * LANES, KV_PER_STEP * tq
    grid_spec = pltpu.PrefetchScalarGridSpec(
        num_scalar_prefetch=2,
        grid=(bsz, ngrp, len(steps)),
        in_specs=[pl.BlockSpec((None, tq, wide), lambda b, p, t, qi, kj: (b, qi[t], p)),
                  pl.BlockSpec((None, tk, wide), lambda b, p, t, qi, kj: (b, kj[t], p)),
                  pl.BlockSpec((None, narrow, tk), lambda b, p, t, qi, kj: (b, p, kj[t])),
                  pl.BlockSpec((None, tq, narrow), lambda b, p, t, qi, kj: (b, qi[t], ngrp + p))],
        out_specs=pl.BlockSpec((None, tq, narrow), lambda b, p, t, qi, kj: (b, qi[t], p)),
        scratch_shapes=[pltpu.VMEM((2 * npair, 1, tq), F32), pltpu.VMEM((2 * npair * ACC_ROWS, tq), F32)],
    )
    return pl.pallas_call(
        functools.partial(_mla_kernel, npair=npair),
        grid_spec=grid_spec,
        out_shape=jax.ShapeDtypeStruct((bsz, seq, B_WIDTH), BF16),
        compiler_params=pltpu.CompilerParams(dimension_semantics=("arbitrary",) * 3,
                                             vmem_limit_bytes=VMEM_LIMIT),
        name="mla",
    )(jnp.asarray(qi), jnp.asarray(kj), qf, kf, vt, gates)


def _out_kernel(o1_ref, o4_ref, o16_ref, l1_ref, l4_ref, l16_ref, gate_ref, b_ref, x_ref,
                wo_ref, expand_ref, lng_ref, lnb_ref, out_ref, slab4_ref, slab16_ref):
    tm = out_ref.shape[0]
    nslab = A_WIDTH // LANES

    def unperm(o_ref, slab_ref):
        d = o_ref.shape[0]
        for r in range(d):
            o = o_ref[r].astype(F32)
            for sl in range(nslab):
                slab_ref[sl, pl.ds(r, tm // d, stride=d), :] = o[:, sl * LANES:(sl + 1) * LANES]

    unperm(o4_ref, slab4_ref)
    unperm(o16_ref, slab16_ref)
    l1, l4, l16 = l1_ref[...], l4_ref[...], l16_ref[...]
    m = jnp.maximum(jnp.maximum(l1, l4), l16)
    e1, e4, e16 = jnp.exp2(l1 - m), jnp.exp2(l4 - m), jnp.exp2(l16 - m)
    rden = 1.0 / (e1 + e4 + e16)
    pad = jnp.zeros((2 * SUBLANES - A_HEADS, tm), F32)
    wts = [jnp.concatenate([e * rden, pad], axis=0).astype(BF16) for e in (e1, e4, e16)]

    def merge(rows):
        expand = lambda wt: lax.dot_general(wt[:, rows], expand_ref[...], TN_DIMS, preferred_element_type=F32)
        slab = lambda ref: jnp.concatenate([ref[sl, rows, :] for sl in range(nslab)], axis=1)
        a = expand(wts[0]) * o1_ref[0, rows, :].astype(F32)
        a = a + expand(wts[1]) * slab(slab4_ref) + expand(wts[2]) * slab(slab16_ref)
        return (a * gate_ref[rows, 0:A_WIDTH].astype(F32)).astype(BF16)

    def project(rows, a):
        return _dot(a, wo_ref[0:A_WIDTH, :]) + _dot(b_ref[rows, :], wo_ref[A_WIDTH:, :])

    def norm(rows, y):
        z = DEEPNORM_ALPHA * x_ref[rows, :] + y
        mu = jnp.mean(z, axis=-1, keepdims=True)
        zc = z - mu
        var = jnp.mean(zc * zc, axis=-1, keepdims=True)
        out_ref[rows, :] = zc * lax.rsqrt(var + LN_EPS) * lng_ref[...] + lnb_ref[...]

    chunks = [slice(k * OUT_CHUNK, (k + 1) * OUT_CHUNK) for k in range(tm // OUT_CHUNK)]
    merged, projected = {}, {}
    for k in range(len(chunks) + 2):
        if k < len(chunks):
            merged[k] = merge(chunks[k])
        if 1 <= k <= len(chunks):
            projected[k - 1] = project(chunks[k - 1], merged.pop(k - 1))
        if k >= 2:
            norm(chunks[k - 2], projected.pop(k - 2))


def _finish(os_, lses, gates, b_out, x, wo, expand, ln_g, ln_b, tm):
    bsz, seq, _ = x.shape
    row = lambda w: pl.BlockSpec((None, tm, w), lambda b, i: (b, i, 0))
    res = lambda d: pl.BlockSpec((None, d, tm // d, A_WIDTH), lambda b, i: (b, 0, i, 0))
    lse = pl.BlockSpec((None, A_HEADS, tm), lambda b, i: (b, 0, i))
    full = lambda a: pl.BlockSpec(a.shape, lambda b, i: (0,) * a.ndim)
    return pl.pallas_call(
        _out_kernel,
        grid=(bsz, seq // tm),
        in_specs=[res(DILATIONS[0]), res(DILATIONS[1]), res(DILATIONS[2]),
                  lse, lse, lse, row(A_WIDTH), row(B_WIDTH), row(D_MODEL),
                  full(wo), full(expand), full(ln_g), full(ln_b)],
        out_specs=row(D_MODEL),
        out_shape=jax.ShapeDtypeStruct((bsz, seq, D_MODEL), F32),
        scratch_shapes=[pltpu.VMEM((A_WIDTH // LANES, tm, LANES), F32)] * 2,
        compiler_params=pltpu.CompilerParams(dimension_semantics=("arbitrary", "arbitrary"),
                                             vmem_limit_bytes=VMEM_LIMIT),
        name="merge_out_ln",
    )(*os_, *lses, gates, b_out, x, wo, expand, ln_g, ln_b)


def _selectors():
    qd, hr = B_NOPE_DIM + B_ROPE_DIM, B_ROPE_DIM // 2
    sel_q = np.zeros((B_HEADS * qd, 2 * B_HEADS * LANES), np.float32)
    sel_k = np.zeros((B_HEADS * LANES, B_HEADS * LANES), np.float32)
    sel_v = np.zeros((B_HEADS * LANES, B_WIDTH), np.float32)
    for h in range(B_HEADS):
        i = np.arange(qd)
        sel_q[h * qd + i, h * LANES + i] = 1.0
        j = np.arange(hr)
        rot0 = B_HEADS * LANES + h * LANES + B_NOPE_DIM
        sel_q[h * qd + B_NOPE_DIM + hr + j, rot0 + j] = -1.0
        sel_q[h * qd + B_NOPE_DIM + j, rot0 + hr + j] = 1.0
        i = np.arange(B_NOPE_DIM)
        sel_k[h * LANES + i, h * LANES + i] = 1.0
        sel_v[h * LANES + B_NOPE_DIM + i, h * B_V_DIM + i] = 1.0
    return jnp.asarray(sel_q, BF16), jnp.asarray(sel_k, BF16), jnp.asarray(sel_v, BF16)


def _layout_weights(w_in, q_norm_g, w_uq, kv_norm_g, w_ukv, w_o):
    head, k_rope, b_gate = w_in[:, :C_TAIL], w_in[:, C_TAIL:C_TAIL + B_ROPE_DIM], w_in[:, C_TAIL + B_ROPE_DIM:]
    win = jnp.pad(jnp.concatenate([head, b_gate, k_rope], axis=1).astype(BF16),
                  ((0, 0), (0, IN_COLS - w_in.shape[1])))
    sel_q, sel_k, sel_v = _selectors()
    place = lambda w, sel: jnp.dot(w.astype(BF16), sel, preferred_element_type=F32).astype(BF16)
    wq = place(w_uq, sel_q)
    wk = place(w_ukv, sel_k)
    wvt = lax.dot_general(sel_v, w_ukv.astype(BF16), (((0,), (1,)), ((), ())),
                          preferred_element_type=F32).astype(BF16)
    expand = np.zeros((2 * SUBLANES, A_WIDTH), np.float32)
    for h in range(A_HEADS):
        expand[h, h * A_HEAD_DIM:(h + 1) * A_HEAD_DIM] = 1.0
    return {"win": win, "wq": wq, "wk": wk, "wvt": wvt,
            "gq": q_norm_g.reshape(1, -1).astype(F32), "gkv": kv_norm_g.reshape(1, -1).astype(F32),
            "wo": w_o.astype(BF16), "expand": jnp.asarray(expand, BF16)}


def _rope_tables(seq):
    inv_freq = ROPE_THETA ** (-np.arange(0, B_ROPE_DIM, 2, dtype=np.float64) / B_ROPE_DIM)
    ang = np.arange(seq, dtype=np.float64)[:, None] * inv_freq[None, :]
    ang = np.concatenate([ang, ang], axis=-1)
    cos, sin = np.cos(ang), np.sin(ang)
    scale = (B_NOPE_DIM + B_ROPE_DIM) ** -0.5 * LOG2E
    one = np.ones((seq, B_NOPE_DIM))
    zn = np.zeros((seq, B_NOPE_DIM))
    zt = np.zeros((seq, LANES - B_NOPE_DIM - B_ROPE_DIM))
    cosq = np.concatenate([one, cos, zt], axis=1) * scale
    sinq = np.concatenate([zn, sin, zt], axis=1) * scale
    csk = np.concatenate([cos, sin, np.zeros((seq, LANES - 2 * B_ROPE_DIM))], axis=1)
    return {"csk": jnp.asarray(csk, F32), "cosq": jnp.asarray(cosq, F32), "sinq": jnp.asarray(sinq, F32)}


def kernel(x, w_in, q_norm_g, w_uq, kv_norm_g, w_ukv, w_o, ln_g, ln_b):
    bsz, seq, _ = x.shape
    wts = _layout_weights(w_in, q_norm_g, w_uq, kv_norm_g, w_ukv, w_o)
    tabs = _rope_tables(seq)
    aqkv1, aqkv4, aqkv16, gates, qf, kf, vt = _project(x, wts, tabs, tm=512)
    os_, lses = [], []
    for aqkv in (aqkv1, aqkv4, aqkv16):
        d, n = aqkv.shape[1], aqkv.shape[2]
        o, lse_t = _dilated(aqkv, tq=min(DILATED_TQ, n))
        os_.append(o)
        lses.append(lse_t.reshape(bsz, A_HEADS, seq) if d == 1 else
                    jnp.transpose(lse_t, (0, 2, 3, 1)).reshape(bsz, A_HEADS, seq))
    b_out = _mla(qf, kf, vt, gates, tq=512, npair=4)
    return _finish(os_, lses, gates, b_out, x, wts["wo"], wts["expand"],
                   ln_g.reshape(1, -1).astype(F32), ln_b.reshape(1, -1).astype(F32), tm=1024)
```

```python
import functools

import jax
import jax.numpy as jnp
import numpy as np
from jax import lax
from jax.experimental import pallas as pl
from jax.experimental.pallas import tpu as pltpu

D_MODEL = 1024
A_HEADS = 8
A_HEAD_DIM = 64
A_WIDTH = A_HEADS * A_HEAD_DIM
DILATED_PATTERNS = ((128, 1), (512, 4), (2048, 16))
DILATIONS = tuple(d for _, d in DILATED_PATTERNS)
BAND = 128
B_HEADS = 8
B_NOPE_DIM = 64
B_ROPE_DIM = 32
B_V_DIM = 64
B_WIDTH = B_HEADS * B_V_DIM
Q_LORA_RANK = 256
KV_LORA_RANK = 128
ROPE_THETA = 10000.0
LN_EPS = 1e-5
RMS_EPS = 1e-6
DEPTH = 1
DEEPNORM_ALPHA = (2 * DEPTH) ** 0.25

LANES = 128
SUBLANES = 8
HEAD_PAIRS = A_HEADS // 2
NEG = -1e30
VMEM_LIMIT = 56 * 1024 * 1024

BF16 = jnp.bfloat16
F32 = jnp.float32

C_AQKV = 0
C_AGATE = 3 * A_WIDTH
C_CQ = 4 * A_WIDTH
C_TAIL = C_CQ + Q_LORA_RANK + KV_LORA_RANK
IN_COLS = 3072
AQKV_SLABS = 3 * A_WIDTH // LANES

TN_DIMS = (((0,), (0,)), ((), ()))
LOG2E = float(np.log2(np.e))
ONES_ROWS = 16
ACC_ROWS = B_V_DIM + ONES_ROWS
KV_PER_STEP = 2
MLA_LOOKAHEAD = 2
OUT_CHUNK = 128
DILATED_GROUP = 8
DILATED_TQ = DILATED_GROUP * BAND


def _dot(a, b):
    return jnp.dot(a, b, preferred_element_type=F32)


def _dot_nt(a, b):
    return lax.dot_general(a, b, (((1,), (1,)), ((), ())), preferred_element_type=F32)


def _rms(t, g):
    return t * lax.rsqrt(jnp.mean(t * t, axis=-1, keepdims=True) + RMS_EPS) * g


def _silu(t):
    return t / (1.0 + jnp.exp(-t))


def _proj_kernel(x_ref, win_ref, wq_ref, wk_ref, wvt_ref, gq_ref, gkv_ref, csk_ref, cosq_ref, sinq_ref,
                 aqkv1_ref, aqkv4_ref, aqkv16_ref, gate_ref, qf_ref, kf_ref, vt_ref, slab_ref, slab4_ref):
    xb = x_ref[...].astype(BF16)
    tm = xb.shape[0]
    lane_tile = lax.broadcasted_iota(jnp.int32, (1, LANES), 1)
    aqkv = _dot(xb, win_ref[:, C_AQKV:C_AGATE])
    aqkv = jnp.concatenate([aqkv[:, :A_WIDTH] * (A_HEAD_DIM ** -0.5 * LOG2E), aqkv[:, A_WIDTH:]], axis=1)
    aqkv1_ref[0] = aqkv.astype(BF16)
    for sl in range(AQKV_SLABS):
        slab_ref[sl] = aqkv[:, sl * LANES:(sl + 1) * LANES]
    d4, d16 = DILATIONS[1], DILATIONS[2]
    n4, n16, step = tm // d4, tm // d16, d16 // d4
    for r in range(d4):
        for sl in range(AQKV_SLABS):
            cols = slice(sl * LANES, (sl + 1) * LANES)
            t = slab_ref[sl, pl.ds(r, n4, stride=d4), :]
            aqkv4_ref[r, :, cols] = t.astype(BF16)
            slab4_ref[sl, r * n4:(r + 1) * n4, :] = t
    for r in range(d4):
        for c in range(step):
            for sl in range(AQKV_SLABS):
                cols = slice(sl * LANES, (sl + 1) * LANES)
                aqkv16_ref[c * d4 + r, :, cols] = slab4_ref[sl, pl.ds(r * n4 + c, n16, stride=step), :].astype(BF16)
    rest = _dot(xb, win_ref[:, C_AGATE:IN_COLS])
    off = lambda c: c - C_AGATE
    gate_ref[:, 0:A_WIDTH] = _silu(rest[:, 0:off(C_CQ)]).astype(BF16)
    gate_ref[:, A_WIDTH:] = _silu(rest[:, off(C_TAIL):off(C_TAIL) + B_WIDTH]).astype(BF16)
    cq = rest[:, off(C_CQ):off(C_CQ) + Q_LORA_RANK]
    ckv = rest[:, off(C_CQ) + Q_LORA_RANK:off(C_TAIL)]
    rk = jnp.where(lane_tile < B_ROPE_DIM, rest[:, off(C_TAIL) + B_WIDTH:], 0.0)
    hr = B_ROPE_DIM // 2
    rope = (rk + jnp.where((lane_tile >= B_ROPE_DIM) & (lane_tile < B_ROPE_DIM + hr), -pltpu.roll(rk, hr, axis=1), 0.0)
            + jnp.where((lane_tile >= B_ROPE_DIM + hr) & (lane_tile < 2 * B_ROPE_DIM),
                        pltpu.roll(rk, B_ROPE_DIM + hr, axis=1), 0.0))
    q2 = _dot(_rms(cq, gq_ref[...]).astype(BF16), wq_ref[...])
    cosq, sinq = jnp.tile(cosq_ref[...], (1, B_HEADS)), jnp.tile(sinq_ref[...], (1, B_HEADS))
    qf_ref[...] = (q2[:, :B_HEADS * LANES] * cosq + q2[:, B_HEADS * LANES:] * sinq).astype(BF16)
    r2 = rope * csk_ref[...]
    kr = pltpu.roll(r2, B_NOPE_DIM, axis=1) + pltpu.roll(r2, B_NOPE_DIM - B_ROPE_DIM, axis=1)
    kr = jnp.where((lane_tile >= B_NOPE_DIM) & (lane_tile < B_NOPE_DIM + B_ROPE_DIM), kr, 0.0)
    kvn = _rms(ckv, gkv_ref[...]).astype(BF16)
    kn = _dot(kvn, wk_ref[...])
    for h in range(B_HEADS):
        kf_ref[:, h * LANES:(h + 1) * LANES] = (kn[:, h * LANES:(h + 1) * LANES] + kr).astype(BF16)
    vt_ref[...] = _dot_nt(wvt_ref[...], kvn).astype(BF16)


def _project(x, wts, tabs, tm):
    bsz, seq, _ = x.shape
    row = lambda w: pl.BlockSpec((None, tm, w), lambda i, b: (b, i, 0))
    res = lambda d: pl.BlockSpec((None, d, tm // d, 3 * A_WIDTH), lambda i, b: (b, 0, i, 0))
    full = lambda a: pl.BlockSpec(a.shape, lambda i, b: (0,) * a.ndim)
    tab = lambda w: pl.BlockSpec((tm, w), lambda i, b: (i, 0))
    out = lambda w: jax.ShapeDtypeStruct((bsz, seq, w), BF16)
    res_out = lambda d: jax.ShapeDtypeStruct((bsz, d, seq // d, 3 * A_WIDTH), BF16)
    return pl.pallas_call(
        _proj_kernel,
        grid=(seq // tm, bsz),
        in_specs=[row(D_MODEL), full(wts["win"]), full(wts["wq"]), full(wts["wk"]), full(wts["wvt"]),
                  full(wts["gq"]), full(wts["gkv"]),
                  tab(LANES), tab(LANES), tab(LANES)],
        out_specs=[res(DILATIONS[0]), res(DILATIONS[1]), res(DILATIONS[2]), row(A_WIDTH + B_WIDTH),
                   row(B_HEADS * LANES), row(B_HEADS * LANES),
                   pl.BlockSpec((None, B_WIDTH, tm), lambda i, b: (b, 0, i))],
        out_shape=[res_out(DILATIONS[0]), res_out(DILATIONS[1]), res_out(DILATIONS[2]), out(A_WIDTH + B_WIDTH),
                   out(B_HEADS * LANES), out(B_HEADS * LANES),
                   jax.ShapeDtypeStruct((bsz, B_WIDTH, seq), BF16)],
        scratch_shapes=[pltpu.VMEM((AQKV_SLABS, tm, LANES), F32)] * 2,
        compiler_params=pltpu.CompilerParams(dimension_semantics=("arbitrary", "arbitrary"),
                                             vmem_limit_bytes=VMEM_LIMIT),
        name="proj",
    )(x, wts["win"], wts["wq"], wts["wk"], wts["wvt"], wts["gq"], wts["gkv"],
      tabs["csk"], tabs["cosq"], tabs["sinq"])


def _dilated_kernel(q_ref, kc_ref, kp_ref, vc_ref, vp_ref, bias_ref, o_ref, lse_ref,
                    kk_ref, vt_ref, qq_ref, *, nsub, group):
    blk = pl.program_id(2)
    nres = q_ref.shape[0]
    lane = lax.broadcasted_iota(jnp.int32, (1, A_WIDTH), 1) % LANES
    lo = lane < A_HEAD_DIM
    for s in range(nres):
        for c in range(nsub):
            q = q_ref[s, c * BAND:(c + 1) * BAND, :]
            qq_ref[s, c, 0:BAND, :] = jnp.where(lo, q, jnp.zeros_like(q))
            qq_ref[s, c, BAND:, :] = jnp.where(lo, jnp.zeros_like(q), q)
        kk_ref[s, 0:BAND, :] = kp_ref[s]
        kk_ref[s, BAND:, :] = kc_ref[s]
        vt_ref[s, :, 0:BAND] = vp_ref[s].T
        vt_ref[s, :, BAND:] = vc_ref[s].T
    top = lax.broadcasted_iota(jnp.int32, (LANES, 1), 0) < A_HEAD_DIM
    first = (blk == 0).astype(jnp.int32)
    pair_cols = [slice(j * LANES, (j + 1) * LANES) for j in range(HEAD_PAIRS)]
    subs = [(s, c) for s in range(nres) for c in range(nsub)]
    for g0 in range(0, len(subs), group):
        blocks = [(s, c, j, slice(c * BAND, (c + 2) * BAND), pair_cols[j])
                  for s, c in subs[g0:g0 + group] for j in range(HEAD_PAIRS)]
        sts = [(_dot_nt(kk_ref[s, keys_rows, cols], qq_ref[s, c, :, cols])
                + bias_ref[first if c == 0 else 0, j]).astype(BF16)
               for s, c, j, keys_rows, cols in blocks]
        ms = [jnp.max(st, axis=0, keepdims=True) for st in sts]
        ps = [jnp.exp2(st - m) for st, m in zip(sts, ms)]
        ms = [m.astype(F32) for m in ms]
        ones = jnp.ones((ONES_ROWS, 2 * BAND), BF16)
        ot2s = [_dot(jnp.concatenate([vt_ref[s, cols, keys_rows], ones], axis=0), p)
                for (s, _, _, keys_rows, cols), p in zip(blocks, ps)]
        lses = {}
        for (s, c, j, _, cols), ot2, m in zip(blocks, ot2s, ms):
            l = ot2[LANES:LANES + 1, :]
            rl = 1.0 / l
            ot = jnp.where(top, ot2[0:LANES, 0:BAND] * rl[:, 0:BAND], ot2[0:LANES, BAND:] * rl[:, BAND:])
            o_ref[s, c * BAND:(c + 1) * BAND, cols] = ot.T.astype(BF16)
            lse = m + jnp.log2(l)
            lses.setdefault((s, c), []).extend([lse[:, 0:BAND], lse[:, BAND:]])
        for (s, c), rows in lses.items():
            lse_ref[s, :, c * BAND:(c + 1) * BAND] = jnp.concatenate(rows, axis=0)


def _dilated_bias(dilation):
    qi = np.arange(BAND)[None, :]
    kj = np.arange(2 * BAND)[:, None]
    rel = qi + BAND - kj
    valid = (rel >= 0) & (rel <= BAND)
    slopes = 2.0 ** (-8.0 * np.arange(1, A_HEADS + 1, dtype=np.float64) / A_HEADS)
    bias = -slopes[:, None, None] * (rel * dilation).astype(np.float64)[None] * LOG2E
    gen = np.where(valid[None], bias, NEG)
    fst = np.where((valid & (kj >= BAND))[None], bias, NEG)
    pair = lambda t: np.concatenate([t[0::2], t[1::2]], axis=-1)
    return jnp.asarray(np.stack([pair(gen), pair(fst)]), F32)


def _dilated(aqkv, tq):
    bsz, dilation, n, _ = aqkv.shape
    nsub = tq // BAND
    nres = min(dilation, DILATED_GROUP // nsub)
    cur = lambda which: pl.BlockSpec((None, nres, tq, A_WIDTH), lambda b, r, i: (b, r, i, which))
    prev = lambda which: pl.BlockSpec((None, nres, BAND, A_WIDTH),
                                      lambda b, r, i: (b, r, jnp.maximum(i * nsub - 1, 0), which))
    bias = _dilated_bias(dilation)
    return pl.pallas_call(
        functools.partial(_dilated_kernel, nsub=nsub, group=DILATED_GROUP),
        grid=(bsz, dilation // nres, n // tq),
        in_specs=[cur(0), cur(1), prev(1), cur(2), prev(2),
                  pl.BlockSpec(bias.shape, lambda b, r, i: (0, 0, 0, 0))],
        out_specs=[pl.BlockSpec((None, nres, tq, A_WIDTH), lambda b, r, i: (b, r, i, 0)),
                   pl.BlockSpec((None, nres, A_HEADS, tq), lambda b, r, i: (b, r, 0, i))],
        out_shape=[jax.ShapeDtypeStruct((bsz, dilation, n, A_WIDTH), BF16),
                   jax.ShapeDtypeStruct((bsz, dilation, A_HEADS, n), F32)],
        scratch_shapes=[pltpu.VMEM((nres, tq + BAND, A_WIDTH), BF16), pltpu.VMEM((nres, A_WIDTH, tq + BAND), BF16),
                        pltpu.VMEM((nres, nsub, 2 * BAND, A_WIDTH), BF16)],
        compiler_params=pltpu.CompilerParams(dimension_semantics=("arbitrary",) * 3,
                                             vmem_limit_bytes=VMEM_LIMIT),
        name=f"dilated_d{dilation}",
    )(aqkv, aqkv, aqkv, aqkv, aqkv, bias)


def _mla_kernel(qi_ref, kj_ref, q_ref, k_ref, vt_ref, g_ref, o_ref, m_ref, acc_ref, *, npair):
    t = pl.program_id(2)
    qi, kj = qi_ref[t], kj_ref[t]
    heads = range(2 * npair)
    acc_rows = lambda h: slice(h * ACC_ROWS, (h + 1) * ACC_ROWS)

    @pl.when(kj == 0)
    def _():
        m_ref[...] = jnp.full(m_ref.shape, NEG, F32)
        acc_ref[...] = jnp.zeros(acc_ref.shape, F32)

    tq = q_ref.shape[0]
    half = tq // 2

    def scores(h, blocks):
        cols = slice(h * LANES, (h + 1) * LANES)
        chunks = []
        for blk, diagonal in blocks:
            k0 = blk * tq
            if not diagonal:
                chunks.append((slice(k0, k0 + tq), 0, _dot_nt(k_ref[k0:k0 + tq, cols], q_ref[:, cols]).astype(BF16)))
                continue
            tri = (lax.broadcasted_iota(jnp.int32, (half, half), 0)
                   <= lax.broadcasted_iota(jnp.int32, (half, half), 1))
            top = _dot_nt(k_ref[k0:k0 + half, cols], q_ref[:, cols])
            top = jnp.concatenate([jnp.where(tri, top[:, 0:half], NEG), top[:, half:]], axis=1)
            bot = jnp.where(tri, _dot_nt(k_ref[k0 + half:k0 + tq, cols], q_ref[half:, cols]), NEG)
            chunks.append((slice(k0, k0 + half), 0, top.astype(BF16)))
            chunks.append((slice(k0 + half, k0 + tq), half, bot.astype(BF16)))
        return chunks

    def accumulate(h, chunks):
        v_ext = lambda keys: jnp.concatenate([vt_ref[h * B_V_DIM:(h + 1) * B_V_DIM, keys],
                                              jnp.ones((ONES_ROWS, keys.stop - keys.start), BF16)], axis=0)
        widen = lambda a, q0, fill: a if q0 == 0 else jnp.concatenate(
            [jnp.full((a.shape[0], q0), fill, a.dtype), a], axis=1)
        m_old = m_ref[h]
        m_new = m_old
        for _, q0, st in chunks:
            m_new = jnp.maximum(m_new, widen(jnp.max(st, axis=0, keepdims=True).astype(F32), q0, NEG))
        m_b = m_new.astype(BF16)
        pv = None
        for keys, q0, st in chunks:
            part = widen(_dot(v_ext(keys), jnp.exp2(st - m_b[:, q0:])), q0, 0.0)
            pv = part if pv is None else pv + part
        alpha = jnp.exp2(m_old - m_new)
        m_ref[h] = m_new
        acc_ref[acc_rows(h), :] = alpha * acc_ref[acc_rows(h), :] + pv

    def step(blocks):
        pending = {}
        for i in range(len(heads) + MLA_LOOKAHEAD):
            if i < len(heads):
                pending[i] = scores(heads[i], blocks)
            if i >= MLA_LOOKAHEAD:
                accumulate(heads[i - MLA_LOOKAHEAD], pending.pop(i - MLA_LOOKAHEAD))

    last = kj == qi // KV_PER_STEP
    odd = qi % KV_PER_STEP == 1

    @pl.when(jnp.logical_not(last))
    def _():
        step([(0, False), (1, False)])

    @pl.when(jnp.logical_and(last, jnp.logical_not(odd)))
    def _():
        step([(0, True)])

    @pl.when(jnp.logical_and(last, odd))
    def _():
        step([(0, False), (1, True)])

    @pl.when(last)
    def _():
        ot = jnp.concatenate([acc_ref[h * ACC_ROWS:h * ACC_ROWS + B_V_DIM, :]
                              * (1.0 / acc_ref[h * ACC_ROWS + B_V_DIM:h * ACC_ROWS + B_V_DIM + 1, :])
                              for h in heads], axis=0)
        o_ref[...] = (ot.T * g_ref[...].astype(F32)).astype(BF16)


def _mla(qf, kf, vt, gates, tq, npair):
    bsz, seq, _ = qf.shape
    nq = seq // tq
    ngrp = HEAD_PAIRS // npair
    steps = [(i, j) for i in range(nq) for j in range(i // KV_PER_STEP + 1)]
    qi = np.asarray([i for i, _ in steps], np.int32)
    kj = np.asarray([j for _, j in steps], np.int32)
    wide, narrow, tk = 2 * npair * LANES, npair * LANES, KV_PER_STEP * tq
    grid_spec = pltpu.PrefetchScalarGridSpec(
        num_scalar_prefetch=2,
        grid=(bsz, ngrp, len(steps)),
        in_specs=[pl.BlockSpec((None, tq, wide), lambda b, p, t, qi, kj: (b, qi[t], p)),
                  pl.BlockSpec((None, tk, wide), lambda b, p, t, qi, kj: (b, kj[t], p)),
                  pl.BlockSpec((None, narrow, tk), lambda b, p, t, qi, kj: (b, p, kj[t])),
                  pl.BlockSpec((None, tq, narrow), lambda b, p, t, qi, kj: (b, qi[t], ngrp + p))],
        out_specs=pl.BlockSpec((None, tq, narrow), lambda b, p, t, qi, kj: (b, qi[t], p)),
        scratch_shapes=[pltpu.VMEM((2 * npair, 1, tq), F32), pltpu.VMEM((2 * npair * ACC_ROWS, tq), F32)],
    )
    return pl.pallas_call(
        functools.partial(_mla_kernel, npair=npair),
        grid_spec=grid_spec,
        out_shape=jax.ShapeDtypeStruct((bsz, seq, B_WIDTH), BF16),
        compiler_params=pltpu.CompilerParams(dimension_semantics=("arbitrary",) * 3,
                                             vmem_limit_bytes=VMEM_LIMIT),
        name="mla",
    )(jnp.asarray(qi), jnp.asarray(kj), qf, kf, vt, gates)


def _out_kernel(o1_ref, o4_ref, o16_ref, l1_ref, l4_ref, l16_ref, gate_ref, b_ref, x_ref,
                wo_ref, expand_ref, lng_ref, lnb_ref, out_ref, slab4_ref, slab16_ref):
    tm = out_ref.shape[0]
    nslab = A_WIDTH // LANES

    def unperm(o_ref, slab_ref):
        d = o_ref.shape[0]
        for r in range(d):
            o = o_ref[r].astype(F32)
            for sl in range(nslab):
                slab_ref[sl, pl.ds(r, tm // d, stride=d), :] = o[:, sl * LANES:(sl + 1) * LANES]

    unperm(o4_ref, slab4_ref)
    unperm(o16_ref, slab16_ref)
    l1, l4, l16 = l1_ref[...], l4_ref[...], l16_ref[...]
    m = jnp.maximum(jnp.maximum(l1, l4), l16)
    e1, e4, e16 = jnp.exp2(l1 - m), jnp.exp2(l4 - m), jnp.exp2(l16 - m)
    rden = 1.0 / (e1 + e4 + e16)
    pad = jnp.zeros((2 * SUBLANES - A_HEADS, tm), F32)
    wts = [jnp.concatenate([e * rden, pad], axis=0).astype(BF16) for e in (e1, e4, e16)]

    def merge(rows):
        expand = lambda wt: lax.dot_general(wt[:, rows], expand_ref[...], TN_DIMS, preferred_element_type=F32)
        slab = lambda ref: jnp.concatenate([ref[sl, rows, :] for sl in range(nslab)], axis=1)
        a = expand(wts[0]) * o1_ref[0, rows, :].astype(F32)
        a = a + expand(wts[1]) * slab(slab4_ref) + expand(wts[2]) * slab(slab16_ref)
        return (a * gate_ref[rows, 0:A_WIDTH].astype(F32)).astype(BF16)

    def project(rows, a):
        return _dot(a, wo_ref[0:A_WIDTH, :]) + _dot(b_ref[rows, :], wo_ref[A_WIDTH:, :])

    def norm(rows, y):
        z = DEEPNORM_ALPHA * x_ref[rows, :] + y
        mu = jnp.mean(z, axis=-1, keepdims=True)
        zc = z - mu
        var = jnp.mean(zc * zc, axis=-1, keepdims=True)
        out_ref[rows, :] = zc * lax.rsqrt(var + LN_EPS) * lng_ref[...] + lnb_ref[...]

    chunks = [slice(k * OUT_CHUNK, (k + 1) * OUT_CHUNK) for k in range(tm // OUT_CHUNK)]
    merged, projected = {}, {}
    for k in range(len(chunks) + 2):
        if k < len(chunks):
            merged[k] = merge(chunks[k])
        if 1 <= k <= len(chunks):
            projected[k - 1] = project(chunks[k - 1], merged.pop(k - 1))
        if k >= 2:
            norm(chunks[k - 2], projected.pop(k - 2))


def _finish(os_, lses, gates, b_out, x, wo, expand, ln_g, ln_b, tm):
    bsz, seq, _ = x.shape
    row = lambda w: pl.BlockSpec((None, tm, w), lambda b, i: (b, i, 0))
    res = lambda d: pl.BlockSpec((None, d, tm // d, A_WIDTH), lambda b, i: (b, 0, i, 0))
    lse = pl.BlockSpec((None, A_HEADS, tm), lambda b, i: (b, 0, i))
    full = lambda a: pl.BlockSpec(a.shape, lambda b, i: (0,) * a.ndim)
    return pl.pallas_call(
        _out_kernel,
        grid=(bsz, seq // tm),
        in_specs=[res(DILATIONS[0]), res(DILATIONS[1]), res(DILATIONS[2]),
                  lse, lse, lse, row(A_WIDTH), row(B_WIDTH), row(D_MODEL),
                  full(wo), full(expand), full(ln_g), full(ln_b)],
        out_specs=row(D_MODEL),
        out_shape=jax.ShapeDtypeStruct((bsz, seq, D_MODEL), F32),
        scratch_shapes=[pltpu.VMEM((A_WIDTH // LANES, tm, LANES), F32)] * 2,
        compiler_params=pltpu.CompilerParams(dimension_semantics=("arbitrary", "arbitrary"),
                                             vmem_limit_bytes=VMEM_LIMIT),
        name="merge_out_ln",
    )(*os_, *lses, gates, b_out, x, wo, expand, ln_g, ln_b)


def _selectors():
    qd, hr = B_NOPE_DIM + B_ROPE_DIM, B_ROPE_DIM // 2
    sel_q = np.zeros((B_HEADS * qd, 2 * B_HEADS * LANES), np.float32)
    sel_k = np.zeros((B_HEADS * LANES, B_HEADS * LANES), np.float32)
    sel_v = np.zeros((B_HEADS * LANES, B_WIDTH), np.float32)
    for h in range(B_HEADS):
        i = np.arange(qd)
        sel_q[h * qd + i, h * LANES + i] = 1.0
        j = np.arange(hr)
        rot0 = B_HEADS * LANES + h * LANES + B_NOPE_DIM
        sel_q[h * qd + B_NOPE_DIM + hr + j, rot0 + j] = -1.0
        sel_q[h * qd + B_NOPE_DIM + j, rot0 + hr + j] = 1.0
        i = np.arange(B_NOPE_DIM)
        sel_k[h * LANES + i, h * LANES + i] = 1.0
        sel_v[h * LANES + B_NOPE_DIM + i, h * B_V_DIM + i] = 1.0
    return jnp.asarray(sel_q, BF16), jnp.asarray(sel_k, BF16), jnp.asarray(sel_v, BF16)


def _layout_weights(w_in, q_norm_g, w_uq, kv_norm_g, w_ukv, w_o):
    head, k_rope, b_gate = w_in[:, :C_TAIL], w_in[:, C_TAIL:C_TAIL + B_ROPE_DIM], w_in[:, C_TAIL + B_ROPE_DIM:]
    win = jnp.pad(jnp.concatenate([head, b_gate, k_rope], axis=1).astype(BF16),
                  ((0, 0), (0, IN_COLS - w_in.shape[1])))
    sel_q, sel_k, sel_v = _selectors()
    place = lambda w, sel: jnp.dot(w.astype(BF16), sel, preferred_element_type=F32).astype(BF16)
    wq = place(w_uq, sel_q)
    wk = place(w_ukv, sel_k)
    wvt = lax.dot_general(sel_v, w_ukv.astype(BF16), (((0,), (1,)), ((), ())),
                          preferred_element_type=F32).astype(BF16)
    expand = np.zeros((2 * SUBLANES, A_WIDTH), np.float32)
    for h in range(A_HEADS):
        expand[h, h * A_HEAD_DIM:(h + 1) * A_HEAD_DIM] = 1.0
    return {"win": win, "wq": wq, "wk": wk, "wvt": wvt,
            "gq": q_norm_g.reshape(1, -1).astype(F32), "gkv": kv_norm_g.reshape(1, -1).astype(F32),
            "wo": w_o.astype(BF16), "expand": jnp.asarray(expand, BF16)}


def _rope_tables(seq):
    inv_freq = ROPE_THETA ** (-np.arange(0, B_ROPE_DIM, 2, dtype=np.float64) / B_ROPE_DIM)
    ang = np.arange(seq, dtype=np.float64)[:, None] * inv_freq[None, :]
    ang = np.concatenate([ang, ang], axis=-1)
    cos, sin = np.cos(ang), np.sin(ang)
    scale = (B_NOPE_DIM + B_ROPE_DIM) ** -0.5 * LOG2E
    one = np.ones((seq, B_NOPE_DIM))
    zn = np.zeros((seq, B_NOPE_DIM))
    zt = np.zeros((seq, LANES - B_NOPE_DIM - B_ROPE_DIM))
    cosq = np.concatenate([one, cos, zt], axis=1) * scale
    sinq = np.concatenate([zn, sin, zt], axis=1) * scale
    csk = np.concatenate([cos, sin, np.zeros((seq, LANES - 2 * B_ROPE_DIM))], axis=1)
    return {"csk": jnp.asarray(csk, F32), "cosq": jnp.asarray(cosq, F32), "sinq": jnp.asarray(sinq, F32)}


def kernel(x, w_in, q_norm_g, w_uq, kv_norm_g, w_ukv, w_o, ln_g, ln_b):
    bsz, seq, _ = x.shape
    wts = _layout_weights(w_in, q_norm_g, w_uq, kv_norm_g, w_ukv, w_o)
    tabs = _rope_tables(seq)
    aqkv1, aqkv4, aqkv16, gates, qf, kf, vt = _project(x, wts, tabs, tm=512)
    os_, lses = [], []
    for aqkv in (aqkv1, aqkv4, aqkv16):
        d, n = aqkv.shape[1], aqkv.shape[2]
        o, lse_t = _dilated(aqkv, tq=min(DILATED_TQ, n))
        os_.append(o)
        lses.append(lse_t.reshape(bsz, A_HEADS, seq) if d == 1 else
                    jnp.transpose(lse_t, (0, 2, 3, 1)).reshape(bsz, A_HEADS, seq))
    b_out = _mla(qf, kf, vt, gates, tq=512, npair=4)
    return _finish(os_, lses, gates, b_out, x, wts["wo"], wts["expand"],
                   ln_g.reshape(1, -1).astype(F32), ln_b.reshape(1, -1).astype(F32), tm=1024)
```

```python
import functools

import jax
import jax.numpy as jnp
import numpy as np
from jax import lax
from jax.experimental import pallas as pl
from jax.experimental.pallas import tpu as pltpu

D_MODEL = 1024
A_HEADS = 8
A_HEAD_DIM = 64
A_WIDTH = A_HEADS * A_HEAD_DIM
DILATED_PATTERNS = ((128, 1), (512, 4), (2048, 16))
DILATIONS = tuple(d for _, d in DILATED_PATTERNS)
BAND = 128
B_HEADS = 8
B_NOPE_DIM = 64
B_ROPE_DIM = 32
B_V_DIM = 64
B_WIDTH = B_HEADS * B_V_DIM
Q_LORA_RANK = 256
KV_LORA_RANK = 128
ROPE_THETA = 10000.0
LN_EPS = 1e-5
RMS_EPS = 1e-6
DEPTH = 1
DEEPNORM_ALPHA = (2 * DEPTH) ** 0.25

LANES = 128
SUBLANES = 8
HEAD_PAIRS = A_HEADS // 2
NEG = -1e30
VMEM_LIMIT = 56 * 1024 * 1024

BF16 = jnp.bfloat16
F32 = jnp.float32

C_AQKV = 0
C_AGATE = 3 * A_WIDTH
C_CQ = 4 * A_WIDTH
C_TAIL = C_CQ + Q_LORA_RANK + KV_LORA_RANK
IN_COLS = 3072
AQKV_SLABS = 3 * A_WIDTH // LANES

TN_DIMS = (((0,), (0,)), ((), ()))
LOG2E = float(np.log2(np.e))
ONES_ROWS = 16
ACC_ROWS = B_V_DIM + ONES_ROWS
KV_PER_STEP = 2
MLA_LOOKAHEAD = 2
OUT_CHUNK = 128
DILATED_GROUP = 8
DILATED_TQ = DILATED_GROUP * BAND


def _dot(a, b):
    return jnp.dot(a, b, preferred_element_type=F32)


def _dot_nt(a, b):
    return lax.dot_general(a, b, (((1,), (1,)), ((), ())), preferred_element_type=F32)


def _rms(t, g):
    return t * lax.rsqrt(jnp.mean(t * t, axis=-1, keepdims=True) + RMS_EPS) * g


def _silu(t):
    return t / (1.0 + jnp.exp(-t))


def _proj_kernel(x_ref, win_ref, wq_ref, wk_ref, wvt_ref, gq_ref, gkv_ref, csk_ref, cosq_ref, sinq_ref,
                 aqkv1_ref, aqkv4_ref, aqkv16_ref, gate_ref, qf_ref, kf_ref, vt_ref, slab_ref, slab4_ref):
    xb = x_ref[...].astype(BF16)
    tm = xb.shape[0]
    lane_tile = lax.broadcasted_iota(jnp.int32, (1, LANES), 1)
    aqkv = _dot(xb, win_ref[:, C_AQKV:C_AGATE])
    aqkv = jnp.concatenate([aqkv[:, :A_WIDTH] * (A_HEAD_DIM ** -0.5 * LOG2E), aqkv[:, A_WIDTH:]], axis=1)
    aqkv1_ref[0] = aqkv.astype(BF16)
    for sl in range(AQKV_SLABS):
        slab_ref[sl] = aqkv[:, sl * LANES:(sl + 1) * LANES]
    d4, d16 = DILATIONS[1], DILATIONS[2]
    n4, n16, step = tm // d4, tm // d16, d16 // d4
    for r in range(d4):
        for sl in range(AQKV_SLABS):
            cols = slice(sl * LANES, (sl + 1) * LANES)
            t = slab_ref[sl, pl.ds(r, n4, stride=d4), :]
            aqkv4_ref[r, :, cols] = t.astype(BF16)
            slab4_ref[sl, r * n4:(r + 1) * n4, :] = t
    for r in range(d4):
        for c in range(step):
            for sl in range(AQKV_SLABS):
                cols = slice(sl * LANES, (sl + 1) * LANES)
                aqkv16_ref[c * d4 + r, :, cols] = slab4_ref[sl, pl.ds(r * n4 + c, n16, stride=step), :].astype(BF16)
    rest = _dot(xb, win_ref[:, C_AGATE:IN_COLS])
    off = lambda c: c - C_AGATE
    gate_ref[:, 0:A_WIDTH] = _silu(rest[:, 0:off(C_CQ)]).astype(BF16)
    gate_ref[:, A_WIDTH:] = _silu(rest[:, off(C_TAIL):off(C_TAIL) + B_WIDTH]).astype(BF16)
    cq = rest[:, off(C_CQ):off(C_CQ) + Q_LORA_RANK]
    ckv = rest[:, off(C_CQ) + Q_LORA_RANK:off(C_TAIL)]
    rk = jnp.where(lane_tile < B_ROPE_DIM, rest[:, off(C_TAIL) + B_WIDTH:], 0.0)
    hr = B_ROPE_DIM // 2
    rope = (rk + jnp.where((lane_tile >= B_ROPE_DIM) & (lane_tile < B_ROPE_DIM + hr), -pltpu.roll(rk, hr, axis=1), 0.0)
            + jnp.where((lane_tile >= B_ROPE_DIM + hr) & (lane_tile < 2 * B_ROPE_DIM),
                        pltpu.roll(rk, B_ROPE_DIM + hr, axis=1), 0.0))
    q2 = _dot(_rms(cq, gq_ref[...]).astype(BF16), wq_ref[...])
    cosq, sinq = jnp.tile(cosq_ref[...], (1, B_HEADS)), jnp.tile(sinq_ref[...], (1, B_HEADS))
    qf_ref[...] = (q2[:, :B_HEADS * LANES] * cosq + q2[:, B_HEADS * LANES:] * sinq).astype(BF16)
    r2 = rope * csk_ref[...]
    kr = pltpu.roll(r2, B_NOPE_DIM, axis=1) + pltpu.roll(r2, B_NOPE_DIM - B_ROPE_DIM, axis=1)
    kr = jnp.where((lane_tile >= B_NOPE_DIM) & (lane_tile < B_NOPE_DIM + B_ROPE_DIM), kr, 0.0)
    kvn = _rms(ckv, gkv_ref[...]).astype(BF16)
    kn = _dot(kvn, wk_ref[...])
    for h in range(B_HEADS):
        kf_ref[:, h * LANES:(h + 1) * LANES] = (kn[:, h * LANES:(h + 1) * LANES] + kr).astype(BF16)
    vt_ref[...] = _dot_nt(wvt_ref[...], kvn).astype(BF16)


def _project(x, wts, tabs, tm):
    bsz, seq, _ = x.shape
    row = lambda w: pl.BlockSpec((None, tm, w), lambda i, b: (b, i, 0))
    res = lambda d: pl.BlockSpec((None, d, tm // d, 3 * A_WIDTH), lambda i, b: (b, 0, i, 0))
    full = lambda a: pl.BlockSpec(a.shape, lambda i, b: (0,) * a.ndim)
    tab = lambda w: pl.BlockSpec((tm, w), lambda i, b: (i, 0))
    out = lambda w: jax.ShapeDtypeStruct((bsz, seq, w), BF16)
    res_out = lambda d: jax.ShapeDtypeStruct((bsz, d, seq // d, 3 * A_WIDTH), BF16)
    return pl.pallas_call(
        _proj_kernel,
        grid=(seq // tm, bsz),
        in_specs=[row(D_MODEL), full(wts["win"]), full(wts["wq"]), full(wts["wk"]), full(wts["wvt"]),
                  full(wts["gq"]), full(wts["gkv"]),
                  tab(LANES), tab(LANES), tab(LANES)],
        out_specs=[res(DILATIONS[0]), res(DILATIONS[1]), res(DILATIONS[2]), row(A_WIDTH + B_WIDTH),
                   row(B_HEADS * LANES), row(B_HEADS * LANES),
                   pl.BlockSpec((None, B_WIDTH, tm), lambda i, b: (b, 0, i))],
        out_shape=[res_out(DILATIONS[0]), res_out(DILATIONS[1]), res_out(DILATIONS[2]), out(A_WIDTH + B_WIDTH),
                   out(B_HEADS * LANES), out(B_HEADS * LANES),
                   jax.ShapeDtypeStruct((bsz, B_WIDTH, seq), BF16)],
        scratch_shapes=[pltpu.VMEM((AQKV_SLABS, tm, LANES), F32)] * 2,
        compiler_params=pltpu.CompilerParams(dimension_semantics=("arbitrary", "arbitrary"),
                                             vmem_limit_bytes=VMEM_LIMIT),
        name="proj",
    )(x, wts["win"], wts["wq"], wts["wk"], wts["wvt"], wts["gq"], wts["gkv"],
      tabs["csk"], tabs["cosq"], tabs["sinq"])


def _dilated_kernel(q_ref, kc_ref, kp_ref, vc_ref, vp_ref, bias_ref, o_ref, lse_ref,
                    kk_ref, vt_ref, qq_ref, *, nsub, group):
    blk = pl.program_id(2)
    nres = q_ref.shape[0]
    lane = lax.broadcasted_iota(jnp.int32, (1, A_WIDTH), 1) % LANES
    lo = lane < A_HEAD_DIM
    for s in range(nres):
        for c in range(nsub):
            q = q_ref[s, c * BAND:(c + 1) * BAND, :]
            qq_ref[s, c, 0:BAND, :] = jnp.where(lo, q, jnp.zeros_like(q))
            qq_ref[s, c, BAND:, :] = jnp.where(lo, jnp.zeros_like(q), q)
        kk_ref[s, 0:BAND, :] = kp_ref[s]
        kk_ref[s, BAND:, :] = kc_ref[s]
        vt_ref[s, :, 0:BAND] = vp_ref[s].T
        vt_ref[s, :, BAND:] = vc_ref[s].T
    top = lax.broadcasted_iota(jnp.int32, (LANES, 1), 0) < A_HEAD_DIM
    first = (blk == 0).astype(jnp.int32)
    pair_cols = [slice(j * LANES, (j + 1) * LANES) for j in range(HEAD_PAIRS)]
    subs = [(s, c) for s in range(nres) for c in range(nsub)]
    for g0 in range(0, len(subs), group):
        blocks = [(s, c, j, slice(c * BAND, (c + 2) * BAND), pair_cols[j])
                  for s, c in subs[g0:g0 + group] for j in range(HEAD_PAIRS)]
        sts = [(_dot_nt(kk_ref[s, keys_rows, cols], qq_ref[s, c, :, cols])
                + bias_ref[first if c == 0 else 0, j]).astype(BF16)
               for s, c, j, keys_rows, cols in blocks]
        ms = [jnp.max(st, axis=0, keepdims=True) for st in sts]
        ps = [jnp.exp2(st - m) for st, m in zip(sts, ms)]
        ms = [m.astype(F32) for m in ms]
        ones = jnp.ones((ONES_ROWS, 2 * BAND), BF16)
        ot2s = [_dot(jnp.concatenate([vt_ref[s, cols, keys_rows], ones], axis=0), p)
                for (s, _, _, keys_rows, cols), p in zip(blocks, ps)]
        lses = {}
        for (s, c, j, _, cols), ot2, m in zip(blocks, ot2s, ms):
            l = ot2[LANES:LANES + 1, :]
            rl = 1.0 / l
            ot = jnp.where(top, ot2[0:LANES, 0:BAND] * rl[:, 0:BAND], ot2[0:LANES, BAND:] * rl[:, BAND:])
            o_ref[s, c * BAND:(c + 1) * BAND, cols] = ot.T.astype(BF16)
            lse = m + jnp.log2(l)
            lses.setdefault((s, c), []).extend([lse[:, 0:BAND], lse[:, BAND:]])
        for (s, c), rows in lses.items():
            lse_ref[s, :, c * BAND:(c + 1) * BAND] = jnp.concatenate(rows, axis=0)


def _dilated_bias(dilation):
    qi = np.arange(BAND)[None, :]
    kj = np.arange(2 * BAND)[:, None]
    rel = qi + BAND - kj
    valid = (rel >= 0) & (rel <= BAND)
    slopes = 2.0 ** (-8.0 * np.arange(1, A_HEADS + 1, dtype=np.float64) / A_HEADS)
    bias = -slopes[:, None, None] * (rel * dilation).astype(np.float64)[None] * LOG2E
    gen = np.where(valid[None], bias, NEG)
    fst = np.where((valid & (kj >= BAND))[None], bias, NEG)
    pair = lambda t: np.concatenate([t[0::2], t[1::2]], axis=-1)
    return jnp.asarray(np.stack([pair(gen), pair(fst)]), F32)


def _dilated(aqkv, tq):
    bsz, dilation, n, _ = aqkv.shape
    nsub = tq // BAND
    nres = min(dilation, DILATED_GROUP // nsub)
    cur = lambda which: pl.BlockSpec((None, nres, tq, A_WIDTH), lambda b, r, i: (b, r, i, which))
    prev = lambda which: pl.BlockSpec((None, nres, BAND, A_WIDTH),
                                      lambda b, r, i: (b, r, jnp.maximum(i * nsub - 1, 0), which))
    bias = _dilated_bias(dilation)
    return pl.pallas_call(
        functools.partial(_dilated_kernel, nsub=nsub, group=DILATED_GROUP),
        grid=(bsz, dilation // nres, n // tq),
        in_specs=[cur(0), cur(1), prev(1), cur(2), prev(2),
                  pl.BlockSpec(bias.shape, lambda b, r, i: (0, 0, 0, 0))],
        out_specs=[pl.BlockSpec((None, nres, tq, A_WIDTH), lambda b, r, i: (b, r, i, 0)),
                   pl.BlockSpec((None, nres, A_HEADS, tq), lambda b, r, i: (b, r, 0, i))],
        out_shape=[jax.ShapeDtypeStruct((bsz, dilation, n, A_WIDTH), BF16),
                   jax.ShapeDtypeStruct((bsz, dilation, A_HEADS, n), F32)],
        scratch_shapes=[pltpu.VMEM((nres, tq + BAND, A_WIDTH), BF16), pltpu.VMEM((nres, A_WIDTH, tq + BAND), BF16),
                        pltpu.VMEM((nres, nsub, 2 * BAND, A_WIDTH), BF16)],
        compiler_params=pltpu.CompilerParams(dimension_semantics=("arbitrary",) * 3,
                                             vmem_limit_bytes=VMEM_LIMIT),
        name=f"dilated_d{dilation}",
    )(aqkv, aqkv, aqkv, aqkv, aqkv, bias)


def _mla_kernel(qi_ref, kj_ref, q_ref, k_ref, vt_ref, g_ref, o_ref, m_ref, acc_ref, *, npair):
    t = pl.program_id(2)
    qi, kj = qi_ref[t], kj_ref[t]
    heads = range(2 * npair)
    acc_rows = lambda h: slice(h * ACC_ROWS, (h + 1) * ACC_ROWS)

    @pl.when(kj == 0)
    def _():
        m_ref[...] = jnp.full(m_ref.shape, NEG, F32)
        acc_ref[...] = jnp.zeros(acc_ref.shape, F32)

    tq = q_ref.shape[0]
    half = tq // 2

    def scores(h, blocks):
        cols = slice(h * LANES, (h + 1) * LANES)
        chunks = []
        for blk, diagonal in blocks:
            k0 = blk * tq
            if not diagonal:
                chunks.append((slice(k0, k0 + tq), 0, _dot_nt(k_ref[k0:k0 + tq, cols], q_ref[:, cols]).astype(BF16)))
                continue
            tri = (lax.broadcasted_iota(jnp.int32, (half, half), 0)
                   <= lax.broadcasted_iota(jnp.int32, (half, half), 1))
            top = _dot_nt(k_ref[k0:k0 + half, cols], q_ref[:, cols])
            top = jnp.concatenate([jnp.where(tri, top[:, 0:half], NEG), top[:, half:]], axis=1)
            bot = jnp.where(tri, _dot_nt(k_ref[k0 + half:k0 + tq, cols], q_ref[half:, cols]), NEG)
            chunks.append((slice(k0, k0 + half), 0, top.astype(BF16)))
            chunks.append((slice(k0 + half, k0 + tq), half, bot.astype(BF16)))
        return chunks

    def accumulate(h, chunks):
        v_ext = lambda keys: jnp.concatenate([vt_ref[h * B_V_DIM:(h + 1) * B_V_DIM, keys],
                                              jnp.ones((ONES_ROWS, keys.stop - keys.start), BF16)], axis=0)
        widen = lambda a, q0, fill: a if q0 == 0 else jnp.concatenate(
            [jnp.full((a.shape[0], q0), fill, a.dtype), a], axis=1)
        m_old = m_ref[h]
        m_new = m_old
        for _, q0, st in chunks:
            m_new = jnp.maximum(m_new, widen(jnp.max(st, axis=0, keepdims=True).astype(F32), q0, NEG))
        m_b = m_new.astype(BF16)
        pv = None
        for keys, q0, st in chunks:
            part = widen(_dot(v_ext(keys), jnp.exp2(st - m_b[:, q0:])), q0, 0.0)
            pv = part if pv is None else pv + part
        alpha = jnp.exp2(m_old - m_new)
        m_ref[h] = m_new
        acc_ref[acc_rows(h), :] = alpha * acc_ref[acc_rows(h), :] + pv

    def step(blocks):
        pending = {}
        for i in range(len(heads) + MLA_LOOKAHEAD):
            if i < len(heads):
                pending[i] = scores(heads[i], blocks)
            if i >= MLA_LOOKAHEAD:
                accumulate(heads[i - MLA_LOOKAHEAD], pending.pop(i - MLA_LOOKAHEAD))

    last = kj == qi // KV_PER_STEP
    odd = qi % KV_PER_STEP == 1

    @pl.when(jnp.logical_not(last))
    def _():
        step([(0, False), (1, False)])

    @pl.when(jnp.logical_and(last, jnp.logical_not(odd)))
    def _():
        step([(0, True)])

    @pl.when(jnp.logical_and(last, odd))
    def _():
        step([(0, False), (1, True)])

    @pl.when(last)
    def _():
        ot = jnp.concatenate([acc_ref[h * ACC_ROWS:h * ACC_ROWS + B_V_DIM, :]
                              * (1.0 / acc_ref[h * ACC_ROWS + B_V_DIM:h * ACC_ROWS + B_V_DIM + 1, :])
                              for h in heads], axis=0)
        o_ref[...] = (ot.T * g_ref[...].astype(F32)).astype(BF16)


def _mla(qf, kf, vt, gates, tq, npair):
    bsz, seq, _ = qf.shape
    nq = seq // tq
    ngrp = HEAD_PAIRS // npair
    steps = [(i, j) for i in range(nq) for j in range(i // KV_PER_STEP + 1)]
    qi = np.asarray([i for i, _ in steps], np.int32)
    kj = np.asarray([j for _, j in steps], np.int32)
    wide, narrow, tk = 2 * npair * LANES, npair * LANES, KV_PER_STEP * tq
    grid_spec = pltpu.PrefetchScalarGridSpec(
        num_scalar_prefetch=2,
        grid=(bsz, ngrp, len(steps)),
        in_specs=[pl.BlockSpec((None, tq, wide), lambda b, p, t, qi, kj: (b, qi[t], p)),
                  pl.BlockSpec((None, tk, wide), lambda b, p, t, qi, kj: (b, kj[t], p)),
                  pl.BlockSpec((None, narrow, tk), lambda b, p, t, qi, kj: (b, p, kj[t])),
                  pl.BlockSpec((None, tq, narrow), lambda b, p, t, qi, kj: (b, qi[t], ngrp + p))],
        out_specs=pl.BlockSpec((None, tq, narrow), lambda b, p, t, qi, kj: (b, qi[t], p)),
        scratch_shapes=[pltpu.VMEM((2 * npair, 1, tq), F32), pltpu.VMEM((2 * npair * ACC_ROWS, tq), F32)],
    )
    return pl.pallas_call(
        functools.partial(_mla_kernel, npair=npair),
        grid_spec=grid_spec,
        out_shape=jax.ShapeDtypeStruct((bsz, seq, B_WIDTH), BF16),
        compiler_params=pltpu.CompilerParams(dimension_semantics=("arbitrary",) * 3,
                                             vmem_limit_bytes=VMEM_LIMIT),
        name="mla",
    )(jnp.asarray(qi), jnp.asarray(kj), qf, kf, vt, gates)


def _out_kernel(o1_ref, o4_ref, o16_ref, l1_ref, l4_ref, l16_ref, gate_ref, b_ref, x_ref,
                wo_ref, expand_ref, lng_ref, lnb_ref, out_ref, slab4_ref, slab16_ref, mid_ref):
    tm = out_ref.shape[0]
    nslab = A_WIDTH // LANES
    d4, d16 = DILATIONS[1], DILATIONS[2]
    n4, n16, step = tm // d4, tm // d16, d16 // d4
    lanes = lambda sl: slice(sl * LANES, (sl + 1) * LANES)
    for r in range(d4):
        o = o4_ref[r].astype(F32)
        for sl in range(nslab):
            slab4_ref[sl, pl.ds(r, n4, stride=d4), :] = o[:, lanes(sl)]
        for c in range(step):
            o = o16_ref[c * d4 + r].astype(F32)
            for sl in range(nslab):
                mid_ref[sl, pl.ds(r * n4 + c, n16, stride=step), :] = o[:, lanes(sl)]
    for r in range(d4):
        for sl in range(nslab):
            slab16_ref[sl, pl.ds(r, n4, stride=d4), :] = mid_ref[sl, r * n4:(r + 1) * n4, :]
    l1, l4, l16 = l1_ref[...], l4_ref[...], l16_ref[...]
    m = jnp.maximum(jnp.maximum(l1, l4), l16)
    e1, e4, e16 = jnp.exp2(l1 - m), jnp.exp2(l4 - m), jnp.exp2(l16 - m)
    rden = 1.0 / (e1 + e4 + e16)
    pad = jnp.zeros((2 * SUBLANES - A_HEADS, tm), F32)
    wts = [jnp.concatenate([e * rden, pad], axis=0).astype(BF16) for e in (e1, e4, e16)]

    def merge(rows):
        expand = lambda wt: lax.dot_general(wt[:, rows], expand_ref[...], TN_DIMS, preferred_element_type=F32)
        slab = lambda ref: jnp.concatenate([ref[sl, rows, :] for sl in range(nslab)], axis=1)
        a = expand(wts[0]) * o1_ref[0, rows, :].astype(F32)
        a = a + expand(wts[1]) * slab(slab4_ref) + expand(wts[2]) * slab(slab16_ref)
        return (a * gate_ref[rows, 0:A_WIDTH].astype(F32)).astype(BF16)

    def project(rows, a):
        return _dot(a, wo_ref[0:A_WIDTH, :]) + _dot(b_ref[rows, :], wo_ref[A_WIDTH:, :])

    def norm(rows, y):
        z = DEEPNORM_ALPHA * x_ref[rows, :] + y
        mu = jnp.mean(z, axis=-1, keepdims=True)
        zc = z - mu
        var = jnp.mean(zc * zc, axis=-1, keepdims=True)
        out_ref[rows, :] = zc * lax.rsqrt(var + LN_EPS) * lng_ref[...] + lnb_ref[...]

    chunks = [slice(k * OUT_CHUNK, (k + 1) * OUT_CHUNK) for k in range(tm // OUT_CHUNK)]
    merged, projected = {}, {}
    for k in range(len(chunks) + 2):
        if k < len(chunks):
            merged[k] = merge(chunks[k])
        if 1 <= k <= len(chunks):
            projected[k - 1] = project(chunks[k - 1], merged.pop(k - 1))
        if k >= 2:
            norm(chunks[k - 2], projected.pop(k - 2))


def _finish(os_, lses, gates, b_out, x, wo, expand, ln_g, ln_b, tm):
    bsz, seq, _ = x.shape
    row = lambda w: pl.BlockSpec((None, tm, w), lambda b, i: (b, i, 0))
    res = lambda d: pl.BlockSpec((None, d, tm // d, A_WIDTH), lambda b, i: (b, 0, i, 0))
    lse = pl.BlockSpec((None, A_HEADS, tm), lambda b, i: (b, 0, i))
    full = lambda a: pl.BlockSpec(a.shape, lambda b, i: (0,) * a.ndim)
    return pl.pallas_call(
        _out_kernel,
        grid=(bsz, seq // tm),
        in_specs=[res(DILATIONS[0]), res(DILATIONS[1]), res(DILATIONS[2]),
                  lse, lse, lse, row(A_WIDTH), row(B_WIDTH), row(D_MODEL),
                  full(wo), full(expand), full(ln_g), full(ln_b)],
        out_specs=row(D_MODEL),
        out_shape=jax.ShapeDtypeStruct((bsz, seq, D_MODEL), F32),
        scratch_shapes=[pltpu.VMEM((A_WIDTH // LANES, tm, LANES), F32)] * 3,
        compiler_params=pltpu.CompilerParams(dimension_semantics=("arbitrary", "arbitrary"),
                                             vmem_limit_bytes=VMEM_LIMIT),
        name="merge_out_ln",
    )(*os_, *lses, gates, b_out, x, wo, expand, ln_g, ln_b)


def _selectors():
    qd, hr = B_NOPE_DIM + B_ROPE_DIM, B_ROPE_DIM // 2
    sel_q = np.zeros((B_HEADS * qd, 2 * B_HEADS * LANES), np.float32)
    sel_k = np.zeros((B_HEADS * LANES, B_HEADS * LANES), np.float32)
    sel_v = np.zeros((B_HEADS * LANES, B_WIDTH), np.float32)
    for h in range(B_HEADS):
        i = np.arange(qd)
        sel_q[h * qd + i, h * LANES + i] = 1.0
        j = np.arange(hr)
        rot0 = B_HEADS * LANES + h * LANES + B_NOPE_DIM
        sel_q[h * qd + B_NOPE_DIM + hr + j, rot0 + j] = -1.0
        sel_q[h * qd + B_NOPE_DIM + j, rot0 + hr + j] = 1.0
        i = np.arange(B_NOPE_DIM)
        sel_k[h * LANES + i, h * LANES + i] = 1.0
        sel_v[h * LANES + B_NOPE_DIM + i, h * B_V_DIM + i] = 1.0
    return jnp.asarray(sel_q, BF16), jnp.asarray(sel_k, BF16), jnp.asarray(sel_v, BF16)


def _layout_weights(w_in, q_norm_g, w_uq, kv_norm_g, w_ukv, w_o):
    head, k_rope, b_gate = w_in[:, :C_TAIL], w_in[:, C_TAIL:C_TAIL + B_ROPE_DIM], w_in[:, C_TAIL + B_ROPE_DIM:]
    win = jnp.pad(jnp.concatenate([head, b_gate, k_rope], axis=1).astype(BF16),
                  ((0, 0), (0, IN_COLS - w_in.shape[1])))
    sel_q, sel_k, sel_v = _selectors()
    place = lambda w, sel: jnp.dot(w.astype(BF16), sel, preferred_element_type=F32).astype(BF16)
    wq = place(w_uq, sel_q)
    wk = place(w_ukv, sel_k)
    wvt = lax.dot_general(sel_v, w_ukv.astype(BF16), (((0,), (1,)), ((), ())),
                          preferred_element_type=F32).astype(BF16)
    expand = np.zeros((2 * SUBLANES, A_WIDTH), np.float32)
    for h in range(A_HEADS):
        expand[h, h * A_HEAD_DIM:(h + 1) * A_HEAD_DIM] = 1.0
    return {"win": win, "wq": wq, "wk": wk, "wvt": wvt,
            "gq": q_norm_g.reshape(1, -1).astype(F32), "gkv": kv_norm_g.reshape(1, -1).astype(F32),
            "wo": w_o.astype(BF16), "expand": jnp.asarray(expand, BF16)}


def _rope_tables(seq):
    inv_freq = ROPE_THETA ** (-np.arange(0, B_ROPE_DIM, 2, dtype=np.float64) / B_ROPE_DIM)
    ang = np.arange(seq, dtype=np.float64)[:, None] * inv_freq[None, :]
    ang = np.concatenate([ang, ang], axis=-1)
    cos, sin = np.cos(ang), np.sin(ang)
    scale = (B_NOPE_DIM + B_ROPE_DIM) ** -0.5 * LOG2E
    one = np.ones((seq, B_NOPE_DIM))
    zn = np.zeros((seq, B_NOPE_DIM))
    zt = np.zeros((seq, LANES - B_NOPE_DIM - B_ROPE_DIM))
    cosq = np.concatenate([one, cos, zt], axis=1) * scale
    sinq = np.concatenate([zn, sin, zt], axis=1) * scale
    csk = np.concatenate([cos, sin, np.zeros((seq, LANES - 2 * B_ROPE_DIM))], axis=1)
    return {"csk": jnp.asarray(csk, F32), "cosq": jnp.asarray(cosq, F32), "sinq": jnp.asarray(sinq, F32)}


def kernel(x, w_in, q_norm_g, w_uq, kv_norm_g, w_ukv, w_o, ln_g, ln_b):
    bsz, seq, _ = x.shape
    wts = _layout_weights(w_in, q_norm_g, w_uq, kv_norm_g, w_ukv, w_o)
    tabs = _rope_tables(seq)
    aqkv1, aqkv4, aqkv16, gates, qf, kf, vt = _project(x, wts, tabs, tm=512)
    os_, lses = [], []
    for aqkv in (aqkv1, aqkv4, aqkv16):
        d, n = aqkv.shape[1], aqkv.shape[2]
        o, lse_t = _dilated(aqkv, tq=min(DILATED_TQ, n))
        os_.append(o)
        lses.append(lse_t.reshape(bsz, A_HEADS, seq) if d == 1 else
                    jnp.transpose(lse_t, (0, 2, 3, 1)).reshape(bsz, A_HEADS, seq))
    b_out = _mla(qf, kf, vt, gates, tq=512, npair=4)
    return _finish(os_, lses, gates, b_out, x, wts["wo"], wts["expand"],
                   ln_g.reshape(1, -1).astype(F32), ln_b.reshape(1, -1).astype(F32), tm=1024)
```
